```python
import math
import jax, jax.numpy as jnp
from jax import lax
import numpy as np

D_MODEL = 2048
BATCH = 4
SEQ = 2048
DEPTH = 2

D_MIX = D_MODEL
SB_HEAD_DIM = 64
SB_WIDTH = D_MIX // 4
SB_HEADS = SB_WIDTH // SB_HEAD_DIM
SSM_WIDTH = D_MIX // 4
SSM_GROUP_CH = 16
SSM_GROUPS = SSM_WIDTH // SSM_GROUP_CH
SSM_STATE = 64
DIFF_WIDTH = D_MIX - SB_WIDTH - SSM_WIDTH
DIFF_HEAD_DIM = 64
DIFF_V_DIM = 2 * DIFF_HEAD_DIM
DIFF_HEADS = DIFF_WIDTH // DIFF_V_DIM
IN_COLS = 3 * SB_WIDTH + SSM_WIDTH + 3 * DIFF_WIDTH
N_GROUPS = 4
EXPERTS_PER_GROUP = 8
N_EXPERTS = N_GROUPS * EXPERTS_PER_GROUP
TOP_K_IN_GROUP = 2
D_EXPERT = D_MODEL // 4
Q_BLOCK = 128
EPS = 1e-6

kernel_name = 'hybrid_sb_s5_diffattn_hmoe'


def rmsnorm(x, g):
    xf = x.astype(jnp.float32)
    y = xf * lax.rsqrt(jnp.mean(xf * xf, axis=-1, keepdims=True) + EPS)
    return (y * g.astype(jnp.float32)).astype(x.dtype)


def stick_breaking_attention(q, k, v):
    seq = q.shape[1]
    scale = q.shape[-1] ** -0.5
    outs = []
    for t0 in range(0, seq, Q_BLOCK):
        t1 = t0 + Q_BLOCK
        z = jnp.einsum('bqhd,bkhd->bhqk', q[:, t0:t1], k[:, :t1]).astype(jnp.float32) * scale
        strict = jnp.arange(t1)[None, :] < (t0 + jnp.arange(Q_BLOCK))[:, None]
        log_fail = jnp.where(strict, jax.nn.log_sigmoid(-z), 0.0)
        log_after = lax.cumsum(log_fail, axis=3, reverse=True) - log_fail
        w = jnp.where(strict, jnp.exp(jax.nn.log_sigmoid(z) + log_after), 0.0)
        outs.append(jnp.einsum('bhqk,bkhd->bqhd', w.astype(v.dtype), v[:, :t1]))
    return jnp.concatenate(outs, axis=1)


def differential_attention(q, k, v, lam):
    seq = q.shape[1]
    scale = q.shape[-1] ** -0.5
    outs = []
    for t0 in range(0, seq, Q_BLOCK):
        t1 = t0 + Q_BLOCK
        s = jnp.einsum('bqhrd,bkhrd->bhrqk', q[:, t0:t1], k[:, :t1]).astype(jnp.float32) * scale
        causal = jnp.arange(t1)[None, :] <= (t0 + jnp.arange(Q_BLOCK))[:, None]
        p = jax.nn.softmax(jnp.where(causal, s, -jnp.inf), axis=-1)
        a = p[:, :, 0] - lam * p[:, :, 1]
        outs.append(jnp.einsum('bhqk,bkhd->bqhd', a.astype(v.dtype), v[:, :t1]))
    return jnp.concatenate(outs, axis=1)


def _diag_combine(e1, e2):
    a1, b1 = e1
    a2, b2 = e2
    return (a1 * a2, a2 * b1 + b2)


def s5_mixer(u, lam_re, lam_im, b_re, b_im, c_re, c_im, d_skip, log_dt, w_glu, b_glu):
    bsz, seq, _ = u.shape
    f32 = jnp.float32
    uf = u.astype(f32).reshape(bsz, seq, SSM_GROUPS, SSM_GROUP_CH)
    lam = lax.complex(lam_re.astype(f32), lam_im.astype(f32))
    dt = jnp.exp(log_dt.astype(f32))[:, None]
    lam_bar = jnp.exp(lam * dt)
    b = lax.complex(b_re.astype(f32), b_im.astype(f32))
    b_bar = ((lam_bar - 1.0) / lam)[:, :, None] * b
    bu = jnp.einsum('blgc,gnc->blgn', uf.astype(jnp.complex64), b_bar)
    a = jnp.broadcast_to(lam_bar, bu.shape)
    _, state = lax.associative_scan(_diag_combine, (a, bu), axis=1)
    c = lax.complex(c_re.astype(f32), c_im.astype(f32))
    y = jnp.einsum('blgn,gcn->blgc', state, c).real + d_skip.astype(f32) * uf
    y = jax.nn.gelu(y.reshape(bsz, seq, SSM_WIDTH))
    y = y * jax.nn.sigmoid(y @ w_glu.astype(f32) + b_glu.astype(f32))
    return y.astype(u.dtype)


def hybrid_mixer(xn, w_in, sb_norm_g, lam_re, lam_im, b_re, b_im, c_re, c_im, d_skip, log_dt,
                 w_glu, b_glu, ssm_norm_g, lq1, lk1, lq2, lk2, diff_subln_g, w_out, lam_init):
    bsz, seq, _ = xn.shape
    f32 = jnp.float32
    proj = xn @ w_in
    cuts = np.cumsum([SB_WIDTH] * 3 + [SSM_WIDTH] + [DIFF_WIDTH] * 2).tolist()
    sb_q, sb_k, sb_v, ssm_u, df_q, df_k, df_v = jnp.split(proj, cuts, axis=-1)
    sb_shape = (bsz, seq, SB_HEADS, SB_HEAD_DIM)
    y_sb = stick_breaking_attention(sb_q.reshape(sb_shape), sb_k.reshape(sb_shape), sb_v.reshape(sb_shape))
    y_sb = rmsnorm(y_sb.reshape(bsz, seq, SB_WIDTH), sb_norm_g)
    y_ssm = rmsnorm(s5_mixer(ssm_u, lam_re, lam_im, b_re, b_im, c_re, c_im, d_skip, log_dt, w_glu, b_glu),
                    ssm_norm_g)
    lam = (jnp.exp(jnp.sum(lq1.astype(f32) * lk1.astype(f32)))
           - jnp.exp(jnp.sum(lq2.astype(f32) * lk2.astype(f32))) + lam_init)
    qk_shape = (bsz, seq, DIFF_HEADS, 2, DIFF_HEAD_DIM)
    y_df = differential_attention(df_q.reshape(qk_shape), df_k.reshape(qk_shape),
                                  df_v.reshape(bsz, seq, DIFF_HEADS, DIFF_V_DIM), lam)
    y_df = (rmsnorm(y_df, diff_subln_g) * (1.0 - lam_init)).reshape(bsz, seq, DIFF_WIDTH)
    y = jnp.concatenate([y_sb, y_ssm, y_df], axis=-1)
    return y @ w_out


def hierarchical_moe(xn, rg_w, rg_b, re_w, re_b, w1, w3, w2):
    bsz, seq, d = xn.shape
    f32 = jnp.float32
    t = xn.reshape(bsz * seq, d)
    g_prob = jax.nn.softmax((t @ rg_w + rg_b).astype(f32), axis=-1)
    g_top, g_idx = lax.top_k(g_prob, 1)
    e_logits_all = (jnp.einsum('td,gde->tge', t, re_w) + re_b).astype(f32)
    e_logits = jnp.take_along_axis(e_logits_all, g_idx[:, :, None], axis=1)[:, 0]
    e_top, e_idx = lax.top_k(jax.nn.softmax(e_logits, axis=-1), TOP_K_IN_GROUP)
    e_w = e_top / jnp.sum(e_top, axis=-1, keepdims=True) * g_top
    expert_id = g_idx * EXPERTS_PER_GROUP + e_idx
    gate = jnp.sum(jax.nn.one_hot(expert_id, N_EXPERTS, dtype=f32) * e_w[..., None], axis=1)
    h = jax.nn.silu(jnp.einsum('td,edf->tef', t, w1)) * jnp.einsum('td,edf->tef', t, w3)
    h = h * gate[:, :, None].astype(h.dtype)
    y = jnp.einsum('tef,efd->td', h, w2)
    return y.reshape(bsz, seq, d)


def setup_inputs(seed: int = 0) -> dict:
    key = jax.random.key(seed)
    ks = jax.random.split(key, 26)
    f32 = jnp.float32

    def nrm(k, shape, scale):
        return jax.random.normal(k, shape, f32) * scale

    L = DEPTH
    G, N, C = SSM_GROUPS, SSM_STATE, SSM_GROUP_CH
    lam_im = jnp.broadcast_to(math.pi * jnp.arange(N, dtype=f32), (L, G, N))
    return {
        'x': nrm(ks[0], (BATCH, SEQ, D_MODEL), 1.0),
        'norm_mix_g': 1.0 + nrm(ks[1], (L, D_MODEL), 0.02),
        'w_in': nrm(ks[2], (L, D_MODEL, IN_COLS), D_MODEL ** -0.5),
        'sb_norm_g': 1.0 + nrm(ks[3], (L, SB_WIDTH), 0.02),
        'ssm_lam_re': -0.5 * jnp.exp(nrm(ks[4], (L, G, N), 0.02)),
        'ssm_lam_im': lam_im,
        'ssm_b_re': nrm(ks[5], (L, G, N, C), (2 * C) ** -0.5),
        'ssm_b_im': nrm(ks[6], (L, G, N, C), (2 * C) ** -0.5),
        'ssm_c_re': nrm(ks[7], (L, G, C, N), N ** -0.5),
        'ssm_c_im': nrm(ks[8], (L, G, C, N), N ** -0.5),
        'ssm_d': nrm(ks[9], (L, G, C), 1.0),
        'ssm_log_dt': jax.random.uniform(ks[10], (L, G), f32, math.log(1e-3), math.log(1e-1)),
        'ssm_w_glu': nrm(ks[11], (L, SSM_WIDTH, SSM_WIDTH), SSM_WIDTH ** -0.5),
        'ssm_b_glu': nrm(ks[12], (L, SSM_WIDTH), 0.01),
        'ssm_norm_g': 1.0 + nrm(ks[13], (L, SSM_WIDTH), 0.02),
        'diff_lq1': nrm(ks[14], (L, DIFF_HEAD_DIM), 0.1),
        'diff_lk1': nrm(ks[15], (L, DIFF_HEAD_DIM), 0.1),
        'diff_lq2': nrm(ks[16], (L, DIFF_HEAD_DIM), 0.1),
        'diff_lk2': nrm(ks[17], (L, DIFF_HEAD_DIM), 0.1),
        'diff_subln_g': 1.0 + nrm(ks[18], (L, DIFF_V_DIM), 0.02),
        'w_out': nrm(ks[19], (L, D_MIX, D_MODEL), D_MIX ** -0.5),
        'norm_ffn_g': 1.0 + nrm(ks[20], (L, D_MODEL), 0.02),
        'router_group_w': nrm(ks[21], (L, D_MODEL, N_GROUPS), D_MODEL ** -0.5),
        'router_group_b': nrm(ks[22], (L, N_GROUPS), 0.01),
        'router_expert_w': nrm(ks[23], (L, N_GROUPS, D_MODEL, EXPERTS_PER_GROUP), D_MODEL ** -0.5),
        'router_expert_b': nrm(ks[24], (L, N_GROUPS, EXPERTS_PER_GROUP), 0.01),
        'expert_w1': nrm(jax.random.fold_in(ks[25], 1), (L, N_EXPERTS, D_MODEL, D_EXPERT), D_MODEL ** -0.5),
        'expert_w3': nrm(jax.random.fold_in(ks[25], 2), (L, N_EXPERTS, D_MODEL, D_EXPERT), D_MODEL ** -0.5),
        'expert_w2': nrm(jax.random.fold_in(ks[25], 3), (L, N_EXPERTS, D_EXPERT, D_MODEL), D_EXPERT ** -0.5),
        'final_norm_g': 1.0 + nrm(jax.random.fold_in(ks[25], 4), (D_MODEL,), 0.02),
    }


def reference(x, norm_mix_g, w_in, sb_norm_g, ssm_lam_re, ssm_lam_im, ssm_b_re, ssm_b_im,
              ssm_c_re, ssm_c_im, ssm_d, ssm_log_dt, ssm_w_glu, ssm_b_glu, ssm_norm_g,
              diff_lq1, diff_lk1, diff_lq2, diff_lk2, diff_subln_g, w_out, norm_ffn_g,
              router_group_w, router_group_b, router_expert_w, router_expert_b,
              expert_w1, expert_w3, expert_w2, final_norm_g):
    for l in range(DEPTH):
        lam_init = 0.8 - 0.6 * math.exp(-0.3 * l)
        h = x + hybrid_mixer(rmsnorm(x, norm_mix_g[l]), w_in[l], sb_norm_g[l],
                             ssm_lam_re[l], ssm_lam_im[l], ssm_b_re[l], ssm_b_im[l],
                             ssm_c_re[l], ssm_c_im[l], ssm_d[l], ssm_log_dt[l],
                             ssm_w_glu[l], ssm_b_glu[l], ssm_norm_g[l],
                             diff_lq1[l], diff_lk1[l], diff_lq2[l], diff_lk2[l],
                             diff_subln_g[l], w_out[l], lam_init)
        x = h + hierarchical_moe(rmsnorm(h, norm_ffn_g[l]), router_group_w[l], router_group_b[l],
                                 router_expert_w[l], router_expert_b[l],
                                 expert_w1[l], expert_w3[l], expert_w2[l])
    return rmsnorm(x, final_norm_g)
```

```python
import functools
import math

import jax
import jax.numpy as jnp
from jax import lax
from jax.experimental import pallas as pl
from jax.experimental.pallas import tpu as pltpu

F32 = jnp.float32
BF16 = jnp.bfloat16

EPS = 1e-6
SB_HEAD_DIM = 64
SSM_GROUP_CH = 16
SSM_STATE = 64
DIFF_HEAD_DIM = 64
N_GROUPS = 4
EXPERTS_PER_GROUP = 8
N_EXPERTS = N_GROUPS * EXPERTS_PER_GROUP

LANES = 128
VMEM_LIMIT = 56 * 1024 * 1024

INPROJ_TM = 512
INPROJ_TN = 1024
ATTN_T = 256
S5_CHUNK = 32
OUT_TM = 256
MOE_TM = 256
FINAL_TM = 256


def _params(*sem):
    return pltpu.CompilerParams(dimension_semantics=sem, vmem_limit_bytes=VMEM_LIMIT)


def _dot(a, b):
    return jnp.dot(a, b, preferred_element_type=F32)


def _dot_nt(a, b):
    return lax.dot_general(a, b, (((1,), (1,)), ((), ())), preferred_element_type=F32)


def _split_bf16(x):
    hi = x.astype(BF16)
    lo = (x - hi.astype(F32)).astype(BF16)
    return hi, lo


def _inproj_body(*refs, n_res):
    res = refs[:n_res]
    g_ref, w_ref = refs[n_res:n_res + 2]
    if n_res > 1:
        proj_ref, xsum_ref, xn_ref = refs[n_res + 2:]
    else:
        proj_ref, xn_ref = refs[n_res + 2:]

    @pl.when(pl.program_id(1) == 0)
    def _():
        x = res[0][...]
        for r in res[1:]:
            x = x + r[...]
        if n_res > 1:
            xsum_ref[...] = x
        ms = jnp.mean(x * x, axis=-1, keepdims=True)
        xn_ref[...] = (x * lax.rsqrt(ms + EPS) * g_ref[...]).astype(BF16)

    proj_ref[...] = _dot(xn_ref[...], w_ref[...]).astype(BF16)


def _norm_inproj(res, row_offsets, t, g, w_all, layer):
    n_res = len(res)
    d = res[0].shape[1]
    n = w_all.shape[-1]
    tm, tn = min(INPROJ_TM, t), INPROJ_TN
    in_specs = [pl.BlockSpec((tm, d), functools.partial(lambda i, j, blk: (i + blk, 0), blk=off // tm))
                for off in row_offsets]
    in_specs += [pl.BlockSpec((1, d), lambda i, j: (0, 0)),
                 pl.BlockSpec((None, d, tn), lambda i, j: (layer, 0, j))]
    out_shape = [jax.ShapeDtypeStruct((t, n), BF16)]
    out_specs = [pl.BlockSpec((tm, tn), lambda i, j: (i, j))]
    if n_res > 1:
        out_shape.append(jax.ShapeDtypeStruct((t, d), F32))
        out_specs.append(pl.BlockSpec((tm, d), lambda i, j: (i, 0)))
    outs = pl.pallas_call(
        functools.partial(_inproj_body, n_res=n_res),
        grid=(t // tm, n // tn),
        in_specs=in_specs, out_specs=out_specs, out_shape=out_shape,
        scratch_shapes=[pltpu.VMEM((tm, d), BF16)],
        compiler_params=_params("arbitrary", "arbitrary"),
        name="norm_inproj",
    )(*res, g, w_all)
    return outs if n_res > 1 else (outs[0], res[0])


def _sb_body(q_ref, k_ref, v_ref, o_ref, *, t):
    qi = pl.program_id(2)
    hd = SB_HEAD_DIM
    lane = lax.broadcasted_iota(jnp.int32, (1, 2 * hd), 1)
    head_lanes = (lane < hd, lane >= hd)
    qs = q_ref[...] * (hd ** -0.5)
    zero = jnp.zeros((), BF16)
    qm = [jnp.where(m, qs, zero) for m in head_lanes]
    row = lax.broadcasted_iota(jnp.int32, (t, t), 0)
    col = lax.broadcasted_iota(jnp.int32, (t, t), 1)
    later = (row > col).astype(BF16)
    strict = col < row

    def block(kb, carry, diag):
        acc, runs = carry
        start = pl.multiple_of(kb * t, t)
        kblk = k_ref[pl.ds(start, t), :]
        vblk = v_ref[pl.ds(start, t), :]
        new_runs = []
        for h in range(2):
            z = _dot_nt(qm[h], kblk)
            sp = jnp.log1p(jnp.exp(-jnp.abs(z)))
            log_hit = jnp.minimum(z, 0.0) - sp
            log_fail = -jnp.maximum(z, 0.0) - sp
            if diag:
                log_fail = jnp.where(strict, log_fail, 0.0)
            hi, lo = _split_bf16(log_fail)
            log_after = _dot(hi, later) + _dot(lo, later)
            w = jnp.exp(log_hit + log_after + runs[h])
            if diag:
                w = jnp.where(strict, w, 0.0)
            vm = jnp.where(head_lanes[h], vblk, zero)
            acc = acc + _dot(w.astype(BF16), vm)
            new_runs.append(runs[h] + jnp.sum(log_fail, axis=-1, keepdims=True))
        return acc, tuple(new_runs)

    zrun = jnp.zeros((t, 1), F32)
    carry = block(qi, (jnp.zeros((t, 2 * hd), F32), (zrun, zrun)), True)
    carry = lax.fori_loop(0, qi, lambda it, c: block(qi - 1 - it, c, False), carry)
    o_ref[...] = carry[0].astype(BF16)


def _sb_attention(proj, bsz, seq, width):
    t = min(ATTN_T, seq)
    nq = seq // t
    npair = width // LANES
    kernel = functools.partial(_sb_body, t=t)
    return pl.pallas_call(
        kernel,
        grid=(bsz, npair, nq),
        in_specs=[pl.BlockSpec((t, LANES), lambda b, p, i: (b * nq + i, p)),
                  pl.BlockSpec((seq, LANES), lambda b, p, i: (b, npair + p)),
                  pl.BlockSpec((seq, LANES), lambda b, p, i: (b, 2 * npair + p))],
        out_specs=pl.BlockSpec((t, LANES), lambda b, p, i: (b * nq + i, p)),
        out_shape=jax.ShapeDtypeStruct((bsz * seq, width), BF16),
        compiler_params=_params("arbitrary", "arbitrary", "arbitrary"),
        name="sb_attention",
    )(proj, proj, proj)


def _diff_body(lq1_ref, lk1_ref, lq2_ref, lk2_ref, g_ref, q_ref, k_ref, v_ref, o_ref, *, t, lam_init):
    qi = pl.program_id(2)
    hd = DIFF_HEAD_DIM
    lam = (jnp.exp(jnp.sum(lq1_ref[...] * lk1_ref[...], axis=-1, keepdims=True))
           - jnp.exp(jnp.sum(lq2_ref[...] * lk2_ref[...], axis=-1, keepdims=True)) + lam_init)
    lane = lax.broadcasted_iota(jnp.int32, (1, 2 * hd), 1)
    qs = q_ref[...] * (hd ** -0.5)
    zero = jnp.zeros((), BF16)
    qm = [jnp.where(lane < hd, qs, zero), jnp.where(lane >= hd, qs, zero)]
    row = lax.broadcasted_iota(jnp.int32, (t, t), 0)
    col = lax.broadcasted_iota(jnp.int32, (t, t), 1)
    causal = col <= row

    def block(kb, carry, diag):
        start = pl.multiple_of(kb * t, t)
        kblk = k_ref[pl.ds(start, t), :]
        vblk = v_ref[pl.ds(start, t), :]
        out = []
        for h in range(2):
            m, l, acc = carry[h]
            s = _dot_nt(qm[h], kblk)
            if diag:
                s = jnp.where(causal, s, -jnp.inf)
            m_new = jnp.maximum(m, jnp.max(s, axis=-1, keepdims=True))
            p = jnp.exp(s - m_new)
            alpha = jnp.exp(m - m_new)
            l = alpha * l + jnp.sum(p, axis=-1, keepdims=True)
            acc = alpha * acc + _dot(p.astype(BF16), vblk)
            out.append((m_new, l, acc))
        return tuple(out)

    init = (jnp.full((t, 1), -jnp.inf, F32), jnp.zeros((t, 1), F32), jnp.zeros((t, 2 * hd), F32))
    carry = block(qi, (init, init), True)
    carry = lax.fori_loop(0, qi, lambda it, c: block(qi - 1 - it, c, False), carry)
    (_, l0, a0), (_, l1, a1) = carry
    o = a0 / l0 - lam * (a1 / l1)
    ms = jnp.mean(o * o, axis=-1, keepdims=True)
    o_ref[...] = (o * lax.rsqrt(ms + EPS) * g_ref[...] * (1.0 - lam_init)).astype(BF16)


def _diff_attention(proj, lam_params, g, bsz, seq, col0, width, lam_init):
    t = min(ATTN_T, seq)
    nq = seq // t
    nh = width // LANES
    c0 = col0 // LANES
    small = pl.BlockSpec((1, DIFF_HEAD_DIM), lambda b, h, i: (0, 0))
    kernel = functools.partial(_diff_body, t=t, lam_init=lam_init)
    return pl.pallas_call(
        kernel,
        grid=(bsz, nh, nq),
        in_specs=[small, small, small, small,
                  pl.BlockSpec((1, LANES), lambda b, h, i: (0, 0)),
                  pl.BlockSpec((t, LANES), lambda b, h, i: (b * nq + i, c0 + h)),
                  pl.BlockSpec((seq, LANES), lambda b, h, i: (b, c0 + nh + h)),
                  pl.BlockSpec((seq, LANES), lambda b, h, i: (b, c0 + 2 * nh + h))],
        out_specs=pl.BlockSpec((t, LANES), lambda b, h, i: (b * nq + i, h)),
        out_shape=jax.ShapeDtypeStruct((bsz * seq, width), BF16),
        compiler_params=_params("arbitrary", "arbitrary", "arbitrary"),
        name="diff_attention",
    )(*lam_params, g, proj, proj, proj)


def _s5_body(u_ref, m_ref, ere_ref, eim_ref, fre_ref, fim_ref, are_ref, aim_ref, y_ref,
             xre_s, xim_s, sre_s, sim_s, *, bsz, nchunk):
    u = u_ref[...]
    xre_s[...] = _dot(u, ere_ref[...])
    xim_s[...] = _dot(u, eim_ref[...])
    are = are_ref[...]
    aim = aim_ref[...]
    sre = jnp.zeros((bsz, SSM_STATE), F32)
    sim = jnp.zeros((bsz, SSM_STATE), F32)
    for j in range(nchunk):
        rows = pl.ds(j * bsz, bsz)
        sre_s[rows, :] = sre
        sim_s[rows, :] = sim
        sre, sim = (are * sre - aim * sim + xre_s[rows, :],
                    are * sim + aim * sre + xim_s[rows, :])
    y_ref[...] = (_dot(u, m_ref[...])
                  + _dot(sre_s[...].astype(BF16), fre_ref[...])
                  + _dot(sim_s[...].astype(BF16), fim_ref[...]))


def _s5_operators(lam_re, lam_im, b_re, b_im, c_re, c_im, d_skip, log_dt, chunk):
    hp = lax.Precision.HIGHEST
    g, n = lam_re.shape
    ch = b_re.shape[-1]
    dt = jnp.exp(log_dt.astype(F32))[:, None]
    k = jnp.arange(chunk + 1, dtype=F32)[None, :, None]
    mag = jnp.exp(k * (lam_re * dt)[:, None, :])
    ang = k * (lam_im * dt)[:, None, :]
    p_re, p_im = mag * jnp.cos(ang), mag * jnp.sin(ang)
    lb_re, lb_im = p_re[:, 1], p_im[:, 1]
    den = lam_re * lam_re + lam_im * lam_im
    q_re = ((lb_re - 1.0) * lam_re + lb_im * lam_im) / den
    q_im = (lb_im * lam_re - (lb_re - 1.0) * lam_im) / den
    bb_re = q_re[:, :, None] * b_re - q_im[:, :, None] * b_im
    bb_im = q_re[:, :, None] * b_im + q_im[:, :, None] * b_re
    pb_re = p_re[:, :, :, None] * bb_re[:, None] - p_im[:, :, :, None] * bb_im[:, None]
    pb_im = p_re[:, :, :, None] * bb_im[:, None] + p_im[:, :, :, None] * bb_re[:, None]
    w = (jnp.einsum('gon,gkni->gkio', c_re, pb_re[:, :chunk], precision=hp)
         - jnp.einsum('gon,gkni->gkio', c_im, pb_im[:, :chunk], precision=hp))
    w = w.at[:, 0].add(jnp.eye(ch, dtype=F32)[None] * d_skip[:, :, None])
    s_idx = jnp.arange(chunk)[:, None]
    t_idx = jnp.arange(chunk)[None, :]
    lag = t_idx - s_idx
    m5 = jnp.where((lag >= 0)[None, :, :, None, None], w[:, jnp.clip(lag, 0, chunk - 1)], 0.0)
    m = m5.transpose(0, 1, 3, 2, 4).reshape(g, chunk * ch, chunk * ch)
    e_re = pb_re[:, :chunk][:, ::-1].transpose(0, 1, 3, 2).reshape(g, chunk * ch, n)
    e_im = pb_im[:, :chunk][:, ::-1].transpose(0, 1, 3, 2).reshape(g, chunk * ch, n)
    cp_re = c_re[:, None] * p_re[:, 1:, None, :] - c_im[:, None] * p_im[:, 1:, None, :]
    cp_im = c_re[:, None] * p_im[:, 1:, None, :] + c_im[:, None] * p_re[:, 1:, None, :]
    f_re = cp_re.transpose(0, 3, 1, 2).reshape(g, n, chunk * ch)
    f_im = (-cp_im).transpose(0, 3, 1, 2).reshape(g, n, chunk * ch)
    a_re = p_re[:, chunk][:, None, :]
    a_im = p_im[:, chunk][:, None, :]
    return (m.astype(BF16), e_re.astype(BF16), e_im.astype(BF16), f_re.astype(BF16), f_im.astype(BF16),
            a_re, a_im)


def _s5_scan(proj, ops, bsz, seq, col0, width):
    m, e_re, e_im, f_re, f_im, a_re, a_im = ops
    g = m.shape[0]
    ch = width // g
    chunk = m.shape[1] // ch
    nchunk = seq // chunk
    rows, cw, n = nchunk * bsz, chunk * ch, SSM_STATE
    u = proj[:, col0:col0 + width].reshape(bsz, nchunk, chunk, g, ch)
    u = u.transpose(3, 1, 0, 2, 4).reshape(g, rows, cw)
    grp = lambda *shape: pl.BlockSpec((None,) + shape, lambda i: (i, 0, 0))
    y = pl.pallas_call(
        functools.partial(_s5_body, bsz=bsz, nchunk=nchunk),
        grid=(g,),
        in_specs=[grp(rows, cw), grp(cw, cw), grp(cw, n), grp(cw, n), grp(n, cw), grp(n, cw),
                  grp(1, n), grp(1, n)],
        out_specs=grp(rows, cw),
        out_shape=jax.ShapeDtypeStruct((g, rows, cw), F32),
        scratch_shapes=[pltpu.VMEM((rows, n), F32)] * 4,
        compiler_params=_params("arbitrary"),
        name="s5_scan",
    )(u, m, e_re, e_im, f_re, f_im, a_re, a_im)
    y = y.reshape(g, nchunk, bsz, chunk, ch).transpose(2, 1, 3, 0, 4)
    return y.reshape(bsz * seq, width)


def _rms(x, g):
    ms = jnp.mean(x * x, axis=-1, keepdims=True)
    return x * lax.rsqrt(ms + EPS) * g


def _outproj_body(ysb_ref, yssm_ref, ydf_ref, x_ref, sbg_ref, wglu_ref, bglu_ref, ssmg_ref, wout_ref,
                  ffng_ref, wrhi_ref, wrlo_ref, rb_ref,
                  h_ref, hn_ref, info_ref, cnt_ref, run_s, *, tm, sbw, ssw):
    step = pl.program_id(0)

    @pl.when(step == 0)
    def _():
        run_s[...] = jnp.zeros_like(run_s)

    ysb = _rms(ysb_ref[...].astype(F32), sbg_ref[...]).astype(BF16)
    y = jax.nn.gelu(yssm_ref[...])
    y = y * jax.nn.sigmoid(_dot(y.astype(BF16), wglu_ref[...]) + bglu_ref[...])
    yssm = _rms(y, ssmg_ref[...]).astype(BF16)
    h = (x_ref[...]
         + _dot(ysb, wout_ref[0:sbw, :])
         + _dot(yssm, wout_ref[sbw:sbw + ssw, :])
         + _dot(ydf_ref[...], wout_ref[sbw + ssw:, :]))
    h_ref[...] = h
    hn = _rms(h, ffng_ref[...])
    hn_ref[...] = hn

    hi, lo = _split_bf16(hn)
    logits = (_dot(hi, wrhi_ref[...]) + _dot(hi, wrlo_ref[...]) + _dot(lo, wrhi_ref[...])) + rb_ref[...]
    lane = lax.broadcasted_iota(jnp.int32, (tm, LANES), 1).astype(F32)
    ninf = -jnp.inf

    def first_max(v):
        m = jnp.max(v, axis=-1, keepdims=True)
        idx = jnp.min(jnp.where(v == m, lane, float(LANES)), axis=-1, keepdims=True)
        return m, idx

    gl = jnp.where(lane < N_GROUPS, logits, ninf)
    gmax, gidx = first_max(gl)
    g_top = 1.0 / jnp.sum(jnp.exp(gl - gmax), axis=-1, keepdims=True)
    group_lo = N_GROUPS + EXPERTS_PER_GROUP * gidx
    in_group = (lane >= group_lo) & (lane < group_lo + EXPERTS_PER_GROUP)
    el = jnp.where(in_group, logits, ninf)
    m1, i1 = first_max(el)
    m2, i2 = first_max(jnp.where(lane == i1, ninf, el))
    r = jnp.exp(m2 - m1)
    w_a = g_top / (1.0 + r)
    w_b = g_top * r / (1.0 + r)
    e_a = i1 - N_GROUPS
    e_b = i2 - N_GROUPS

    oh_a = (lane == e_a).astype(F32)
    oh_b = (lane == e_b).astype(F32)
    cnt = oh_a + oh_b
    trow = lax.broadcasted_iota(jnp.int32, (tm, tm), 0)
    tcol = lax.broadcasted_iota(jnp.int32, (tm, tm), 1)
    before = (tcol < trow).astype(BF16)
    base = _dot(before, cnt.astype(BF16)) + run_s[...]
    rank_a = jnp.sum(oh_a * base, axis=-1, keepdims=True)
    rank_b = jnp.sum(oh_b * base, axis=-1, keepdims=True)
    run_s[...] = run_s[...] + jnp.sum(cnt, axis=0, keepdims=True)
    cnt_ref[...] = run_s[...]

    info = jnp.zeros((tm, LANES), F32)
    for k, val in enumerate((e_a, e_b, w_a, w_b, rank_a, rank_b)):
        info = jnp.where(lane == k, val, info)
    info_ref[...] = info


def _outproj_router(ysb, yssm, ydf, x, sbg, wglu, bglu, ssmg, wout_all, ffng, wr_hi, wr_lo, rb, layer):
    t, d = x.shape
    tm = min(OUT_TM, t)
    sbw, ssw, dfw = ysb.shape[1], yssm.shape[1], ydf.shape[1]
    rowblk = lambda w: pl.BlockSpec((tm, w), lambda i: (i, 0))
    const = lambda *shape: pl.BlockSpec(shape, lambda i: (0,) * len(shape))
    kernel = functools.partial(_outproj_body, tm=tm, sbw=sbw, ssw=ssw)
    return pl.pallas_call(
        kernel,
        grid=(t // tm,),
        in_specs=[rowblk(sbw), rowblk(ssw), rowblk(dfw), rowblk(d),
                  const(1, sbw), const(ssw, ssw), const(1, ssw), const(1, ssw),
                  pl.BlockSpec((None, d, d), lambda i: (layer, 0, 0)),
                  const(1, d), const(d, LANES), const(d, LANES), const(1, LANES)],
        out_specs=[rowblk(d), rowblk(d), rowblk(LANES), const(1, LANES)],
        out_shape=[jax.ShapeDtypeStruct((t, d), F32), jax.ShapeDtypeStruct((t, d), F32),
                   jax.ShapeDtypeStruct((t, LANES), F32), jax.ShapeDtypeStruct((1, LANES), F32)],
        scratch_shapes=[pltpu.VMEM((1, LANES), F32)],
        compiler_params=_params("arbitrary"),
        name="outproj_router",
    )(ysb, yssm, ydf, x, sbg, wglu, bglu, ssmg, wout_all, ffng, wr_hi, wr_lo, rb)


def _moe_body(te_ref, nu_ref, src_ref, dst_ref,
              hn_hbm, gate_ref, w1_ref, w3_ref, w2_ref,
              y_hbm,
              xbuf, ybuf, w1b, w3b, w2b, gsem, ssem, *, tm):
    i = pl.program_id(0)
    n_used = nu_ref[0]

    def gather_copy(tile, slot, r):
        tok = src_ref[tile * tm + r]
        return pltpu.make_async_copy(hn_hbm.at[pl.ds(tok, 1), :], xbuf.at[slot, pl.ds(r, 1), :], gsem.at[slot])

    def scatter_copy(tile, r):
        dst = dst_ref[tile * tm + r]
        return dst, pltpu.make_async_copy(ybuf.at[pl.ds(r, 1), :], y_hbm.at[pl.ds(jnp.maximum(dst, 0), 1), :],
                                          ssem.at[0])

    def gather_start(tile, slot):
        def body(r, c):
            gather_copy(tile, slot, r).start()
            return c
        lax.fori_loop(0, tm, body, 0)

    def gather_wait(tile, slot):
        def body(r, c):
            gather_copy(tile, slot, r).wait()
            return c
        lax.fori_loop(0, tm, body, 0)

    def scatter_start(tile):
        def body(r, c):
            dst, cp = scatter_copy(tile, r)

            @pl.when(dst >= 0)
            def _():
                cp.start()
            return c
        lax.fori_loop(0, tm, body, 0)

    def scatter_wait(tile):
        def body(r, c):
            dst, cp = scatter_copy(tile, r)

            @pl.when(dst >= 0)
            def _():
                cp.wait()
            return c
        lax.fori_loop(0, tm, body, 0)

    @pl.when(i == 0)
    def _():
        gather_start(0, 0)

    @pl.when(i < n_used)
    def _():
        slot = i % 2

        @pl.when(i + 1 < n_used)
        def _():
            gather_start(i + 1, 1 - slot)

        new_expert = jnp.logical_or(i == 0, te_ref[i] != te_ref[jnp.maximum(i - 1, 0)])

        @pl.when(new_expert)
        def _():
            w1b[...] = w1_ref[...].astype(BF16)
            w3b[...] = w3_ref[...].astype(BF16)
            w2b[...] = w2_ref[...].astype(BF16)

        gather_wait(i, slot)
        x = xbuf[slot].astype(BF16)
        h1 = _dot(x, w1b[...])
        h3 = _dot(x, w3b[...])
        a = (h1 * jax.nn.sigmoid(h1) * h3 * gate_ref[...]).astype(BF16)
        y = _dot(a, w2b[...])

        @pl.when(i > 0)
        def _():
            scatter_wait(i - 1)

        ybuf[...] = y
        scatter_start(i)

        @pl.when(i == n_used - 1)
        def _():
            scatter_wait(i)


def _moe_experts(hn, tile_expert, n_used, src_tok, dst_row, gate_sorted, w1_all, w3_all, w2_all, layer):
    t, d = hn.shape
    tm = MOE_TM
    n_tiles = tile_expert.shape[0]
    f = w1_all.shape[-1]
    wspec = lambda a, b: pl.BlockSpec((None, None, a, b), lambda i, te, nu, src, dst: (layer, te[i], 0, 0))
    grid_spec = pltpu.PrefetchScalarGridSpec(
        num_scalar_prefetch=4,
        grid=(n_tiles,),
        in_specs=[pl.BlockSpec(memory_space=pl.ANY),
                  pl.BlockSpec((tm, 1), lambda i, te, nu, src, dst: (i, 0)),
                  wspec(d, f), wspec(d, f), wspec(f, d)],
        out_specs=pl.BlockSpec(memory_space=pl.ANY),
        scratch_shapes=[pltpu.VMEM((2, tm, d), F32), pltpu.VMEM((tm, d), F32),
                        pltpu.VMEM((d, f), BF16), pltpu.VMEM((d, f), BF16), pltpu.VMEM((f, d), BF16),
                        pltpu.SemaphoreType.DMA((2,)), pltpu.SemaphoreType.DMA((1,))],
    )
    return pl.pallas_call(
        functools.partial(_moe_body, tm=tm),
        grid_spec=grid_spec,
        out_shape=jax.ShapeDtypeStruct((2 * t, d), F32),
        compiler_params=_params("arbitrary"),
        name="moe_experts",
    )(tile_expert, n_used, src_tok, dst_row, hn, gate_sorted, w1_all, w3_all, w2_all)


def _moe_schedule(info, counts, t):
    tm = MOE_TM
    n_tiles = (2 * t) // tm + N_EXPERTS
    p = n_tiles * tm
    e_id = info[:, 0:2].astype(jnp.int32)
    gate = info[:, 2:4]
    rank = info[:, 4:6].astype(jnp.int32)
    cnt = counts[0, :N_EXPERTS].astype(jnp.int32)
    tiles_e = (cnt + tm - 1) // tm
    tile_end = jnp.cumsum(tiles_e)
    tile_start = tile_end - tiles_e
    n_used = tile_end[-1]
    pos = (tile_start * tm)[e_id] + rank
    tok = jnp.broadcast_to(jnp.arange(t, dtype=jnp.int32)[:, None], (t, 2))
    dst = tok + jnp.array([0, t], jnp.int32)[None, :]
    flat = pos.reshape(-1)
    src_tok = jnp.zeros((p,), jnp.int32).at[flat].set(tok.reshape(-1))
    dst_row = jnp.full((p,), -1, jnp.int32).at[flat].set(dst.reshape(-1))
    gate_sorted = jnp.zeros((p,), F32).at[flat].set(gate.reshape(-1)).reshape(p, 1)
    tile_ids = jnp.minimum(jnp.arange(n_tiles, dtype=jnp.int32), n_used - 1)
    tile_expert = jnp.sum(tile_ids[:, None] >= tile_end[None, :], axis=1).astype(jnp.int32)
    return tile_expert, n_used.reshape(1).astype(jnp.int32), src_tok, dst_row, gate_sorted


def _final_body(h_ref, y0_ref, y1_ref, g_ref, o_ref):
    o_ref[...] = _rms(h_ref[...] + y0_ref[...] + y1_ref[...], g_ref[...])


def _final_norm(h, y, g):
    t, d = h.shape
    tm = min(FINAL_TM, t)
    nblk = t // tm
    return pl.pallas_call(
        _final_body,
        grid=(nblk,),
        in_specs=[pl.BlockSpec((tm, d), lambda i: (i, 0)),
                  pl.BlockSpec((tm, d), lambda i: (i, 0)),
                  pl.BlockSpec((tm, d), lambda i: (i + nblk, 0)),
                  pl.BlockSpec((1, d), lambda i: (0, 0))],
        out_specs=pl.BlockSpec((tm, d), lambda i: (i, 0)),
        out_shape=jax.ShapeDtypeStruct((t, d), F32),
        compiler_params=_params("arbitrary"),
        name="final_norm",
    )(h, y, y, g)


def kernel(x, norm_mix_g, w_in, sb_norm_g, ssm_lam_re, ssm_lam_im, ssm_b_re, ssm_b_im, ssm_c_re, ssm_c_im, ssm_d, ssm_log_dt, ssm_w_glu, ssm_b_glu, ssm_norm_g, diff_lq1, diff_lk1, diff_lq2, diff_lk2, diff_subln_g, w_out, norm_ffn_g, router_group_w, router_group_b, router_expert_w, router_expert_b, expert_w1, expert_w3, expert_w2, final_norm_g):
    bsz, seq, d = x.shape
    depth = w_in.shape[0]
    t = bsz * seq
    sbw = sb_norm_g.shape[-1]
    ssw = ssm_norm_g.shape[-1]
    dfw = d - sbw - ssw
    ssm_col = 3 * sbw
    diff_col = ssm_col + ssw

    w_in_b = w_in.astype(BF16)
    w_out_b = w_out.astype(BF16)
    w_glu_b = ssm_w_glu.astype(BF16)
    row = lambda v: v.reshape(1, -1).astype(F32)

    h = x.reshape(t, d)
    y = None
    for l in range(depth):
        lam_init = 0.8 - 0.6 * math.exp(-0.3 * l)
        if l == 0:
            proj, xres = _norm_inproj([h], [0], t, row(norm_mix_g[l]), w_in_b, l)
        else:
            proj, xres = _norm_inproj([h, y, y], [0, 0, t], t, row(norm_mix_g[l]), w_in_b, l)
        ysb = _sb_attention(proj, bsz, seq, sbw)
        ydf = _diff_attention(proj, [row(p[l]) for p in (diff_lq1, diff_lk1, diff_lq2, diff_lk2)],
                              row(diff_subln_g[l]), bsz, seq, diff_col, dfw, lam_init)
        ops = _s5_operators(ssm_lam_re[l], ssm_lam_im[l], ssm_b_re[l], ssm_b_im[l], ssm_c_re[l], ssm_c_im[l],
                            ssm_d[l], ssm_log_dt[l], S5_CHUNK)
        yssm = _s5_scan(proj, ops, bsz, seq, ssm_col, ssw)

        wr = jnp.zeros((d, LANES), F32)
        wr = wr.at[:, :N_GROUPS].set(router_group_w[l])
        wr = wr.at[:, N_GROUPS:N_GROUPS + N_EXPERTS].set(
            router_expert_w[l].transpose(1, 0, 2).reshape(d, N_EXPERTS))
        wr_hi, wr_lo = _split_bf16(wr)
        rb = jnp.zeros((1, LANES), F32)
        rb = rb.at[0, :N_GROUPS].set(router_group_b[l])
        rb = rb.at[0, N_GROUPS:N_GROUPS + N_EXPERTS].set(router_expert_b[l].reshape(-1))

        h, hn, info, counts = _outproj_router(
            ysb, yssm, ydf, xres, row(sb_norm_g[l]), w_glu_b[l], row(ssm_b_glu[l]), row(ssm_norm_g[l]),
            w_out_b, row(norm_ffn_g[l]), wr_hi, wr_lo, rb, l)
        sched = _moe_schedule(info, counts, t)
        y = _moe_experts(hn, *sched, expert_w1, expert_w3, expert_w2, l)
    out = _final_norm(h, y, row(final_norm_g))
    return out.reshape(bsz, seq, d)
```

```python
import functools
import math

import jax
import jax.numpy as jnp
from jax import lax
from jax.experimental import pallas as pl
from jax.experimental.pallas import tpu as pltpu

F32 = jnp.float32
BF16 = jnp.bfloat16

EPS = 1e-6
SB_HEAD_DIM = 64
SSM_GROUP_CH = 16
SSM_STATE = 64
DIFF_HEAD_DIM = 64
N_GROUPS = 4
EXPERTS_PER_GROUP = 8
N_EXPERTS = N_GROUPS * EXPERTS_PER_GROUP

LANES = 128
INFO_EXPERT, INFO_GATE, INFO_RANK = 0, 2, 4
VMEM_LIMIT = 56 * 1024 * 1024

INPROJ_TM = 512
INPROJ_TN = 1024
SB_TQ = 512
SB_TK = 256
DIFF_T = 512
S5_CHUNK = 32
OUT_TM = 256
MOE_TM = 256
FINAL_TM = 256


def _params(*sem):
    return pltpu.CompilerParams(dimension_semantics=sem, vmem_limit_bytes=VMEM_LIMIT)


def _dot(a, b):
    return jnp.dot(a, b, preferred_element_type=F32)


def _dot_nt(a, b):
    return lax.dot_general(a, b, (((1,), (1,)), ((), ())), preferred_element_type=F32)


def _split_bf16(x):
    hi = x.astype(BF16)
    lo = (x - hi.astype(F32)).astype(BF16)
    return hi, lo


def _moe_combine(h_ref, y0_ref, y1_ref, info_ref):
    info = info_ref[...]
    return (h_ref[...] + info[:, INFO_GATE:INFO_GATE + 1] * y0_ref[...]
            + info[:, INFO_GATE + 1:INFO_GATE + 2] * y1_ref[...])


def _inproj_body(*refs, after_moe):
    n_in = 4 if after_moe else 1
    g_ref, w_ref = refs[n_in:n_in + 2]
    if after_moe:
        proj_ref, x_out_ref, xn_ref = refs[n_in + 2:]
    else:
        proj_ref, xn_ref = refs[n_in + 2:]

    @pl.when(pl.program_id(1) == 0)
    def _():
        if after_moe:
            x = _moe_combine(*refs[:4])
            x_out_ref[...] = x
        else:
            x = refs[0][...]
        ms = jnp.mean(x * x, axis=-1, keepdims=True)
        xn_ref[...] = (x * lax.rsqrt(ms + EPS) * g_ref[...]).astype(BF16)

    proj_ref[...] = _dot(xn_ref[...], w_ref[...]).astype(BF16)


def _norm_inproj(x, moe_out, g, w_all, layer):
    t, d = x.shape
    n = w_all.shape[-1]
    tm, tn = min(INPROJ_TM, t), INPROJ_TN
    rows = lambda blk, w: pl.BlockSpec((tm, w), lambda i, j: (i + blk, 0))
    args, in_specs = [x], [rows(0, d)]
    if moe_out is not None:
        y, info = moe_out
        args += [y, y, info]
        in_specs += [rows(0, d), rows(t // tm, d), rows(0, LANES)]
    in_specs += [pl.BlockSpec((1, d), lambda i, j: (0, 0)),
                 pl.BlockSpec((None, d, tn), lambda i, j: (layer, 0, j))]
    out_shape = [jax.ShapeDtypeStruct((t, n), BF16)]
    out_specs = [pl.BlockSpec((tm, tn), lambda i, j: (i, j))]
    if moe_out is not None:
        out_shape.append(jax.ShapeDtypeStruct((t, d), F32))
        out_specs.append(pl.BlockSpec((tm, d), lambda i, j: (i, 0)))
    outs = pl.pallas_call(
        functools.partial(_inproj_body, after_moe=moe_out is not None),
        grid=(t // tm, n // tn),
        in_specs=in_specs, out_specs=out_specs, out_shape=out_shape,
        scratch_shapes=[pltpu.VMEM((tm, d), BF16)],
        compiler_params=_params("arbitrary", "arbitrary"),
        name="norm_inproj",
    )(*args, g, w_all)
    return outs if moe_out is not None else (outs[0], x)


def _sb_body(q_ref, k_ref, v_ref, o_ref, *, tq, tk):
    qi = pl.program_id(2)
    hd = SB_HEAD_DIM
    nsub = tq // tk
    lane = lax.broadcasted_iota(jnp.int32, (1, 2 * hd), 1)
    head_lanes = (lane < hd, lane >= hd)
    qs = q_ref[...] * (hd ** -0.5)
    zero = jnp.zeros((), BF16)
    qm = [jnp.where(m, qs, zero) for m in head_lanes]
    later = (lax.broadcasted_iota(jnp.int32, (tk, tk), 0)
             > lax.broadcasted_iota(jnp.int32, (tk, tk), 1)).astype(BF16)
    q_pos = qi * tq + lax.broadcasted_iota(jnp.int32, (tq, tk), 0)
    k_off = lax.broadcasted_iota(jnp.int32, (tq, tk), 1)

    def block(kb, carry, diag):
        acc, runs = carry
        start = pl.multiple_of(kb * tk, tk)
        kblk = k_ref[pl.ds(start, tk), :]
        vblk = v_ref[pl.ds(start, tk), :]
        if diag:
            strict = kb * tk + k_off < q_pos
        new_runs = []
        for h in range(2):
            z = _dot_nt(qm[h], kblk)
            log_fail = -jnp.maximum(z, 0.0) - jnp.log(1.0 + jnp.exp(-jnp.abs(z)))
            log_hit = log_fail + z
            if diag:
                log_fail = jnp.where(strict, log_fail, 0.0)
            hi, lo = _split_bf16(log_fail)
            log_after = _dot(hi, later) + _dot(lo, later)
            w = jnp.exp(log_hit + log_after + runs[h])
            if diag:
                w = jnp.where(strict, w, 0.0)
            vm = jnp.where(head_lanes[h], vblk, zero)
            acc = acc + _dot(w.astype(BF16), vm)
            new_runs.append(runs[h] + (log_after[:, 0:1] + log_fail[:, 0:1]))
        return acc, tuple(new_runs)

    zrun = jnp.zeros((tq, 1), F32)
    carry = (jnp.zeros((tq, 2 * hd), F32), (zrun, zrun))
    for sub in reversed(range(nsub)):
        carry = block(nsub * qi + sub, carry, True)
    carry = lax.fori_loop(0, nsub * qi, lambda it, c: block(nsub * qi - 1 - it, c, False), carry)
    o_ref[...] = carry[0].astype(BF16)


def _sb_attention(proj, bsz, seq, width):
    tq, tk = min(SB_TQ, seq), min(SB_TK, seq)
    nq = seq // tq
    npair = width // LANES
    kernel = functools.partial(_sb_body, tq=tq, tk=tk)
    return pl.pallas_call(
        kernel,
        grid=(bsz, npair, nq),
        in_specs=[pl.BlockSpec((tq, LANES), lambda b, p, i: (b * nq + i, p)),
                  pl.BlockSpec((seq, LANES), lambda b, p, i: (b, npair + p)),
                  pl.BlockSpec((seq, LANES), lambda b, p, i: (b, 2 * npair + p))],
        out_specs=pl.BlockSpec((tq, LANES), lambda b, p, i: (b * nq + i, p)),
        out_shape=jax.ShapeDtypeStruct((bsz * seq, width), BF16),
        compiler_params=_params("arbitrary", "arbitrary", "arbitrary"),
        name="sb_attention",
    )(proj, proj, proj)


def _diff_body(lq1_ref, lk1_ref, lq2_ref, lk2_ref, g_ref, q_ref, k_ref, v_ref, o_ref, s_scr, *, t, lam_init):
    qi = pl.program_id(2)
    hd = DIFF_HEAD_DIM
    lam = (jnp.exp(jnp.sum(lq1_ref[...] * lk1_ref[...], axis=-1, keepdims=True))
           - jnp.exp(jnp.sum(lq2_ref[...] * lk2_ref[...], axis=-1, keepdims=True)) + lam_init)
    lane = lax.broadcasted_iota(jnp.int32, (1, 2 * hd), 1)
    qs = q_ref[...] * (hd ** -0.5)
    zero = jnp.zeros((), BF16)
    qm = [jnp.where(lane < hd, qs, zero), jnp.where(lane >= hd, qs, zero)]
    row = lax.broadcasted_iota(jnp.int32, (t, t), 0)
    col = lax.broadcasted_iota(jnp.int32, (t, t), 1)
    causal = col <= row

    def score_block(kb, mx, diag):
        kblk = k_ref[pl.ds(pl.multiple_of(kb * t, t), t), :]
        out = []
        for h in range(2):
            s = _dot_nt(qm[h], kblk)
            if diag:
                s = jnp.where(causal, s, -jnp.inf)
            s_scr[h, kb] = s
            smax = mx[h]
            for c in range(t // LANES):
                smax = jnp.maximum(smax, s[:, c * LANES:(c + 1) * LANES])
            out.append(smax)
        return tuple(out)

    mx0 = jnp.full((t, LANES), -jnp.inf, F32)
    mx = score_block(qi, (mx0, mx0), True)
    mx = lax.fori_loop(0, qi, lambda kb, c: score_block(kb, c, False), mx)
    m = [jnp.broadcast_to(jnp.max(mx[h], axis=-1, keepdims=True), (t, t)) for h in range(2)]
    ones = jnp.ones((t, LANES), BF16)

    def pv_block(kb, acc):
        vaug = jnp.concatenate([v_ref[pl.ds(pl.multiple_of(kb * t, t), t), :], ones], axis=1)
        return tuple(acc[h] + _dot(jnp.exp(s_scr[h, kb] - m[h]).astype(BF16), vaug) for h in range(2))

    acc0 = jnp.zeros((t, 2 * LANES), F32)
    a0, a1 = lax.fori_loop(0, qi + 1, pv_block, (acc0, acc0))
    o = a0[:, :LANES] / a0[:, LANES:LANES + 1] - lam * (a1[:, :LANES] / a1[:, LANES:LANES + 1])
    ms = jnp.mean(o * o, axis=-1, keepdims=True)
    o_ref[...] = (o * lax.rsqrt(ms + EPS) * g_ref[...] * (1.0 - lam_init)).astype(BF16)


def _diff_attention(proj, lam_params, g, bsz, seq, col0, width, lam_init):
    t = min(DIFF_T, seq)
    nq = seq // t
    nh = width // LANES
    c0 = col0 // LANES
    small = pl.BlockSpec((1, DIFF_HEAD_DIM), lambda b, h, i: (0, 0))
    kernel = functools.partial(_diff_body, t=t, lam_init=lam_init)
    return pl.pallas_call(
        kernel,
        grid=(bsz, nh, nq),
        in_specs=[small, small, small, small,
                  pl.BlockSpec((1, LANES), lambda b, h, i: (0, 0)),
                  pl.BlockSpec((t, LANES), lambda b, h, i: (b * nq + i, c0 + h)),
                  pl.BlockSpec((seq, LANES), lambda b, h, i: (b, c0 + nh + h)),
                  pl.BlockSpec((seq, LANES), lambda b, h, i: (b, c0 + 2 * nh + h))],
        out_specs=pl.BlockSpec((t, LANES), lambda b, h, i: (b * nq + i, h)),
        out_shape=jax.ShapeDtypeStruct((bsz * seq, width), BF16),
        scratch_shapes=[pltpu.VMEM((2, nq, t, t), F32)],
        compiler_params=_params("arbitrary", "arbitrary", "arbitrary"),
        name="diff_attention",
    )(*lam_params, g, proj, proj, proj)


def _s5_body(u_ref, m_ref, ere_ref, eim_ref, fre_ref, fim_ref, are_ref, aim_ref, y_ref,
             xre_s, xim_s, sre_s, sim_s, *, bsz, nchunk):
    u = u_ref[...]
    xre_s[...] = _dot(u, ere_ref[...])
    xim_s[...] = _dot(u, eim_ref[...])
    are = are_ref[...]
    aim = aim_ref[...]
    sre = jnp.zeros((bsz, SSM_STATE), F32)
    sim = jnp.zeros((bsz, SSM_STATE), F32)
    for j in range(nchunk):
        rows = pl.ds(j * bsz, bsz)
        sre_s[rows, :] = sre
        sim_s[rows, :] = sim
        sre, sim = (are * sre - aim * sim + xre_s[rows, :],
                    are * sim + aim * sre + xim_s[rows, :])
    y_ref[...] = (_dot(u, m_ref[...])
                  + _dot(sre_s[...].astype(BF16), fre_ref[...])
                  + _dot(sim_s[...].astype(BF16), fim_ref[...]))


def _s5_operators(lam_re, lam_im, b_re, b_im, c_re, c_im, d_skip, log_dt, chunk):
    hp = lax.Precision.HIGHEST
    g, n = lam_re.shape
    ch = b_re.shape[-1]
    dt = jnp.exp(log_dt.astype(F32))[:, None]
    k = jnp.arange(chunk + 1, dtype=F32)[None, :, None]
    mag = jnp.exp(k * (lam_re * dt)[:, None, :])
    ang = k * (lam_im * dt)[:, None, :]
    p_re, p_im = mag * jnp.cos(ang), mag * jnp.sin(ang)
    lb_re, lb_im = p_re[:, 1], p_im[:, 1]
    den = lam_re * lam_re + lam_im * lam_im
    q_re = ((lb_re - 1.0) * lam_re + lb_im * lam_im) / den
    q_im = (lb_im * lam_re - (lb_re - 1.0) * lam_im) / den
    bb_re = q_re[:, :, None] * b_re - q_im[:, :, None] * b_im
    bb_im = q_re[:, :, None] * b_im + q_im[:, :, None] * b_re
    pb_re = p_re[:, :, :, None] * bb_re[:, None] - p_im[:, :, :, None] * bb_im[:, None]
    pb_im = p_re[:, :, :, None] * bb_im[:, None] + p_im[:, :, :, None] * bb_re[:, None]
    w = (jnp.einsum('gon,gkni->gkio', c_re, pb_re[:, :chunk], precision=hp)
         - jnp.einsum('gon,gkni->gkio', c_im, pb_im[:, :chunk], precision=hp))
    w = w.at[:, 0].add(jnp.eye(ch, dtype=F32)[None] * d_skip[:, :, None])
    s_idx = jnp.arange(chunk)[:, None]
    t_idx = jnp.arange(chunk)[None, :]
    lag = t_idx - s_idx
    m5 = jnp.where((lag >= 0)[None, :, :, None, None], w[:, jnp.clip(lag, 0, chunk - 1)], 0.0)
    m = m5.transpose(0, 1, 3, 2, 4).reshape(g, chunk * ch, chunk * ch)
    e_re = pb_re[:, :chunk][:, ::-1].transpose(0, 1, 3, 2).reshape(g, chunk * ch, n)
    e_im = pb_im[:, :chunk][:, ::-1].transpose(0, 1, 3, 2).reshape(g, chunk * ch, n)
    cp_re = c_re[:, None] * p_re[:, 1:, None, :] - c_im[:, None] * p_im[:, 1:, None, :]
    cp_im = c_re[:, None] * p_im[:, 1:, None, :] + c_im[:, None] * p_re[:, 1:, None, :]
    f_re = cp_re.transpose(0, 3, 1, 2).reshape(g, n, chunk * ch)
    f_im = (-cp_im).transpose(0, 3, 1, 2).reshape(g, n, chunk * ch)
    a_re = p_re[:, chunk][:, None, :]
    a_im = p_im[:, chunk][:, None, :]
    return (m.astype(BF16), e_re.astype(BF16), e_im.astype(BF16), f_re.astype(BF16), f_im.astype(BF16),
            a_re, a_im)


def _s5_scan(proj, ops, bsz, seq, col0, width):
    m, e_re, e_im, f_re, f_im, a_re, a_im = ops
    g = m.shape[0]
    ch = width // g
    chunk = m.shape[1] // ch
    nchunk = seq // chunk
    rows, cw, n = nchunk * bsz, chunk * ch, SSM_STATE
    u = proj[:, col0:col0 + width].reshape(bsz, nchunk, chunk, g, ch)
    u = u.transpose(3, 1, 0, 2, 4).reshape(g, rows, cw)
    grp = lambda *shape: pl.BlockSpec((None,) + shape, lambda i: (i, 0, 0))
    y = pl.pallas_call(
        functools.partial(_s5_body, bsz=bsz, nchunk=nchunk),
        grid=(g,),
        in_specs=[grp(rows, cw), grp(cw, cw), grp(cw, n), grp(cw, n), grp(n, cw), grp(n, cw),
                  grp(1, n), grp(1, n)],
        out_specs=grp(rows, cw),
        out_shape=jax.ShapeDtypeStruct((g, rows, cw), F32),
        scratch_shapes=[pltpu.VMEM((rows, n), F32)] * 4,
        compiler_params=_params("arbitrary"),
        name="s5_scan",
    )(u, m, e_re, e_im, f_re, f_im, a_re, a_im)
    y = y.reshape(g, nchunk, bsz, chunk, ch).transpose(2, 1, 3, 0, 4)
    return y.reshape(bsz * seq, width)


def _rms(x, g):
    ms = jnp.mean(x * x, axis=-1, keepdims=True)
    return x * lax.rsqrt(ms + EPS) * g


def _outproj_body(ysb_ref, yssm_ref, ydf_ref, x_ref, sbg_ref, wglu_ref, bglu_ref, ssmg_ref, wout_ref,
                  ffng_ref, wrhi_ref, wrlo_ref, rb_ref,
                  h_ref, hn_ref, info_ref, cnt_ref, run_s, *, tm, sbw, ssw):
    step = pl.program_id(0)

    @pl.when(step == 0)
    def _():
        run_s[...] = jnp.zeros_like(run_s)

    ysb = _rms(ysb_ref[...].astype(F32), sbg_ref[...]).astype(BF16)
    y = jax.nn.gelu(yssm_ref[...])
    y = y * jax.nn.sigmoid(_dot(y.astype(BF16), wglu_ref[...]) + bglu_ref[...])
    yssm = _rms(y, ssmg_ref[...]).astype(BF16)
    h = (x_ref[...]
         + _dot(ysb, wout_ref[0:sbw, :])
         + _dot(yssm, wout_ref[sbw:sbw + ssw, :])
         + _dot(ydf_ref[...], wout_ref[sbw + ssw:, :]))
    h_ref[...] = h
    hn = _rms(h, ffng_ref[...])
    hn_ref[...] = hn

    hi, lo = _split_bf16(hn)
    logits = (_dot(hi, wrhi_ref[...]) + _dot(hi, wrlo_ref[...]) + _dot(lo, wrhi_ref[...])) + rb_ref[...]
    lane = lax.broadcasted_iota(jnp.int32, (tm, LANES), 1).astype(F32)
    ninf = -jnp.inf

    def first_max(v):
        m = jnp.max(v, axis=-1, keepdims=True)
        idx = jnp.min(jnp.where(v == m, lane, float(LANES)), axis=-1, keepdims=True)
        return m, idx

    gl = jnp.where(lane < N_GROUPS, logits, ninf)
    gmax, gidx = first_max(gl)
    g_top = 1.0 / jnp.sum(jnp.exp(gl - gmax), axis=-1, keepdims=True)
    group_lo = N_GROUPS + EXPERTS_PER_GROUP * gidx
    in_group = (lane >= group_lo) & (lane < group_lo + EXPERTS_PER_GROUP)
    el = jnp.where(in_group, logits, ninf)
    m1, i1 = first_max(el)
    m2, i2 = first_max(jnp.where(lane == i1, ninf, el))
    r = jnp.exp(m2 - m1)
    w_a = g_top / (1.0 + r)
    w_b = g_top * r / (1.0 + r)
    e_a = i1 - N_GROUPS
    e_b = i2 - N_GROUPS

    oh_a = (lane == e_a).astype(F32)
    oh_b = (lane == e_b).astype(F32)
    cnt = oh_a + oh_b
    trow = lax.broadcasted_iota(jnp.int32, (tm, tm), 0)
    tcol = lax.broadcasted_iota(jnp.int32, (tm, tm), 1)
    before = (tcol < trow).astype(BF16)
    base = _dot(before, cnt.astype(BF16)) + run_s[...]
    rank_a = jnp.sum(oh_a * base, axis=-1, keepdims=True)
    rank_b = jnp.sum(oh_b * base, axis=-1, keepdims=True)
    run_s[...] = run_s[...] + jnp.sum(cnt, axis=0, keepdims=True)
    cnt_ref[...] = run_s[...]

    info = jnp.zeros((tm, LANES), F32)
    for k, val in ((INFO_EXPERT, e_a), (INFO_EXPERT + 1, e_b), (INFO_GATE, w_a), (INFO_GATE + 1, w_b),
                   (INFO_RANK, rank_a), (INFO_RANK + 1, rank_b)):
        info = jnp.where(lane == k, val, info)
    info_ref[...] = info


def _outproj_router(ysb, yssm, ydf, x, sbg, wglu, bglu, ssmg, wout_all, ffng, wr_hi, wr_lo, rb, layer):
    t, d = x.shape
    tm = min(OUT_TM, t)
    sbw, ssw, dfw = ysb.shape[1], yssm.shape[1], ydf.shape[1]
    rowblk = lambda w: pl.BlockSpec((tm, w), lambda i: (i, 0))
    const = lambda *shape: pl.BlockSpec(shape, lambda i: (0,) * len(shape))
    kernel = functools.partial(_outproj_body, tm=tm, sbw=sbw, ssw=ssw)
    return pl.pallas_call(
        kernel,
        grid=(t // tm,),
        in_specs=[rowblk(sbw), rowblk(ssw), rowblk(dfw), rowblk(d),
                  const(1, sbw), const(ssw, ssw), const(1, ssw), const(1, ssw),
                  pl.BlockSpec((None, d, d), lambda i: (layer, 0, 0)),
                  const(1, d), const(d, LANES), const(d, LANES), const(1, LANES)],
        out_specs=[rowblk(d), rowblk(d), rowblk(LANES), const(1, LANES)],
        out_shape=[jax.ShapeDtypeStruct((t, d), F32), jax.ShapeDtypeStruct((t, d), F32),
                   jax.ShapeDtypeStruct((t, LANES), F32), jax.ShapeDtypeStruct((1, LANES), F32)],
        scratch_shapes=[pltpu.VMEM((1, LANES), F32)],
        compiler_params=_params("arbitrary"),
        name="outproj_router",
    )(ysb, yssm, ydf, x, sbg, wglu, bglu, ssmg, wout_all, ffng, wr_hi, wr_lo, rb)


def _moe_body(te_ref, nu_ref, src_ref, dst_ref,
              hn_hbm, w1_ref, w3_ref, w2_ref,
              y_hbm,
              xbuf0, xbuf1, ybuf0, ybuf1, w1b, w3b, w2b, gsem, ssem, *, tm):
    i = pl.program_id(0)
    n_used = nu_ref[0]
    xbufs = (xbuf0, xbuf1)
    ybufs = (ybuf0, ybuf1)

    def gather_rows(tile, par, start):
        base = tile * tm
        for r in range(tm):
            tok = src_ref[base + r]
            cp = pltpu.make_async_copy(hn_hbm.at[pl.ds(tok, 1), :], xbufs[par].at[pl.ds(r, 1), :], gsem.at[par])
            cp.start() if start else cp.wait()

    def scatter_rows(tile, par, start):
        base = (tile + 1) * tm
        for r in range(tm):
            dst = dst_ref[base + r]
            cp = pltpu.make_async_copy(ybufs[par].at[pl.ds(r, 1), :], y_hbm.at[pl.ds(dst, 1), :], ssem.at[par])
            cp.start() if start else cp.wait()

    @pl.when(i == 0)
    def _():
        ybuf1[...] = jnp.zeros_like(ybuf1)
        gather_rows(0, 0, True)
        gather_rows(0, 0, False)

    new_expert = jnp.logical_or(i == 0, te_ref[i] != te_ref[jnp.maximum(i - 1, 0)])

    @pl.when(jnp.logical_and(i < n_used, new_expert))
    def _():
        w1b[...] = w1_ref[...].astype(BF16)
        w3b[...] = w3_ref[...].astype(BF16)
        w2b[...] = w2_ref[...].astype(BF16)

    for par in range(2):
        active = jnp.logical_and(i < n_used, i % 2 == par)
        nxt = jnp.minimum(i + 1, n_used - 1)

        @pl.when(active)
        def _(par=par, nxt=nxt):
            x = xbufs[par][...].astype(BF16)
            gather_rows(nxt, 1 - par, True)
            scatter_rows(i - 1, 1 - par, True)
            h1 = _dot(x, w1b[...])
            h3 = _dot(x, w3b[...])
            a = (h1 * jax.nn.sigmoid(h1) * h3).astype(BF16)
            ybufs[par][...] = _dot(a, w2b[...])

        @pl.when(active)
        def _(par=par, nxt=nxt):
            scatter_rows(i - 1, 1 - par, False)
            gather_rows(nxt, 1 - par, False)

        @pl.when(jnp.logical_and(active, i == n_used - 1))
        def _(par=par):
            scatter_rows(i, par, True)
            scatter_rows(i, par, False)


def _moe_experts(hn, tile_expert, n_used, src_tok, dst_row, w1_all, w3_all, w2_all, layer):
    t, d = hn.shape
    tm = MOE_TM
    n_tiles = tile_expert.shape[0]
    f = w1_all.shape[-1]
    wspec = lambda a, b: pl.BlockSpec((None, None, a, b), lambda i, te, nu, src, dst: (layer, te[i], 0, 0))
    grid_spec = pltpu.PrefetchScalarGridSpec(
        num_scalar_prefetch=4,
        grid=(n_tiles,),
        in_specs=[pl.BlockSpec(memory_space=pl.ANY), wspec(d, f), wspec(d, f), wspec(f, d)],
        out_specs=pl.BlockSpec(memory_space=pl.ANY),
        scratch_shapes=[pltpu.VMEM((tm, d), F32)] * 4
                       + [pltpu.VMEM((d, f), BF16), pltpu.VMEM((d, f), BF16), pltpu.VMEM((f, d), BF16),
                          pltpu.SemaphoreType.DMA((2,)), pltpu.SemaphoreType.DMA((2,))],
    )
    return pl.pallas_call(
        functools.partial(_moe_body, tm=tm),
        grid_spec=grid_spec,
        out_shape=jax.ShapeDtypeStruct((2 * t + tm, d), F32),
        compiler_params=_params("arbitrary"),
        name="moe_experts",
    )(tile_expert, n_used, src_tok, dst_row, hn, w1_all, w3_all, w2_all)


def _moe_schedule(info, counts, t):
    tm = MOE_TM
    n_tiles = (2 * t) // tm + N_EXPERTS
    p = n_tiles * tm
    e_id = info[:, INFO_EXPERT:INFO_EXPERT + 2].astype(jnp.int32)
    rank = info[:, INFO_RANK:INFO_RANK + 2].astype(jnp.int32)
    cnt = counts[0, :N_EXPERTS].astype(jnp.int32)
    tiles_e = (cnt + tm - 1) // tm
    tile_end = jnp.cumsum(tiles_e)
    tile_start = tile_end - tiles_e
    n_used = tile_end[-1]
    pos = (tile_start * tm)[e_id] + rank
    tok = jnp.broadcast_to(jnp.arange(t, dtype=jnp.int32)[:, None], (t, 2))
    dst = tok + jnp.array([0, t], jnp.int32)[None, :]
    trash = 2 * t + jnp.arange(p, dtype=jnp.int32) % tm
    default = jnp.stack([jnp.zeros((p,), jnp.int32), trash], axis=1)
    update = jnp.stack([tok.reshape(-1), dst.reshape(-1)], axis=1)
    table = default.at[pos.reshape(-1)].set(update)
    src_tok = table[:, 0]
    dst_row = jnp.concatenate([trash[:tm], table[:, 1]])
    tile_ids = jnp.minimum(jnp.arange(n_tiles, dtype=jnp.int32), n_used - 1)
    tile_expert = jnp.sum(tile_ids[:, None] >= tile_end[None, :], axis=1).astype(jnp.int32)
    return tile_expert, n_used.reshape(1).astype(jnp.int32), src_tok, dst_row


def _final_body(h_ref, y0_ref, y1_ref, info_ref, g_ref, o_ref):
    o_ref[...] = _rms(_moe_combine(h_ref, y0_ref, y1_ref, info_ref), g_ref[...])


def _final_norm(h, y, info, g):
    t, d = h.shape
    tm = min(FINAL_TM, t)
    nblk = t // tm
    return pl.pallas_call(
        _final_body,
        grid=(nblk,),
        in_specs=[pl.BlockSpec((tm, d), lambda i: (i, 0)),
                  pl.BlockSpec((tm, d), lambda i: (i, 0)),
                  pl.BlockSpec((tm, d), lambda i: (i + nblk, 0)),
                  pl.BlockSpec((tm, LANES), lambda i: (i, 0)),
                  pl.BlockSpec((1, d), lambda i: (0, 0))],
        out_specs=pl.BlockSpec((tm, d), lambda i: (i, 0)),
        out_shape=jax.ShapeDtypeStruct((t, d), F32),
        compiler_params=_params("arbitrary"),
        name="final_norm",
    )(h, y, y, info, g)


def kernel(x, norm_mix_g, w_in, sb_norm_g, ssm_lam_re, ssm_lam_im, ssm_b_re, ssm_b_im, ssm_c_re, ssm_c_im, ssm_d, ssm_log_dt, ssm_w_glu, ssm_b_glu, ssm_norm_g, diff_lq1, diff_lk1, diff_lq2, diff_lk2, diff_subln_g, w_out, norm_ffn_g, router_group_w, router_group_b, router_expert_w, router_expert_b, expert_w1, expert_w3, expert_w2, final_norm_g):
    bsz, seq, d = x.shape
    depth = w_in.shape[0]
    t = bsz * seq
    sbw = sb_norm_g.shape[-1]
    ssw = ssm_norm_g.shape[-1]
    dfw = d - sbw - ssw
    ssm_col = 3 * sbw
    diff_col = ssm_col + ssw

    w_in_b = w_in.astype(BF16)
    w_out_b = w_out.astype(BF16)
    w_glu_b = ssm_w_glu.astype(BF16)
    row = lambda v: v.reshape(1, -1).astype(F32)

    h = x.reshape(t, d)
    moe_out = None
    for l in range(depth):
        lam_init = 0.8 - 0.6 * math.exp(-0.3 * l)
        proj, xres = _norm_inproj(h, moe_out, row(norm_mix_g[l]), w_in_b, l)
        ysb = _sb_attention(proj, bsz, seq, sbw)
        ydf = _diff_attention(proj, [row(p[l]) for p in (diff_lq1, diff_lk1, diff_lq2, diff_lk2)],
                              row(diff_subln_g[l]), bsz, seq, diff_col, dfw, lam_init)
        ops = _s5_operators(ssm_lam_re[l], ssm_lam_im[l], ssm_b_re[l], ssm_b_im[l], ssm_c_re[l], ssm_c_im[l],
                            ssm_d[l], ssm_log_dt[l], S5_CHUNK)
        yssm = _s5_scan(proj, ops, bsz, seq, ssm_col, ssw)

        wr = jnp.zeros((d, LANES), F32)
        wr = wr.at[:, :N_GROUPS].set(router_group_w[l])
        wr = wr.at[:, N_GROUPS:N_GROUPS + N_EXPERTS].set(
            router_expert_w[l].transpose(1, 0, 2).reshape(d, N_EXPERTS))
        wr_hi, wr_lo = _split_bf16(wr)
        rb = jnp.zeros((1, LANES), F32)
        rb = rb.at[0, :N_GROUPS].set(router_group_b[l])
        rb = rb.at[0, N_GROUPS:N_GROUPS + N_EXPERTS].set(router_expert_b[l].reshape(-1))

        h, hn, info, counts = _outproj_router(
            ysb, yssm, ydf, xres, row(sb_norm_g[l]), w_glu_b[l], row(ssm_b_glu[l]), row(ssm_norm_g[l]),
            w_out_b, row(norm_ffn_g[l]), wr_hi, wr_lo, rb, l)
        sched = _moe_schedule(info, counts, t)
        moe_out = (_moe_experts(hn, *sched, expert_w1, expert_w3, expert_w2, l), info)
    out = _final_norm(h, *moe_out, row(final_norm_g))
    return out.reshape(bsz, seq, d)
```

```python
import functools
import math

import jax
import jax.numpy as jnp
from jax import lax
from jax.experimental import pallas as pl
from jax.experimental.pallas import tpu as pltpu

F32 = jnp.float32
BF16 = jnp.bfloat16

EPS = 1e-6
SB_HEAD_DIM = 64
SSM_GROUP_CH = 16
SSM_STATE = 64
DIFF_HEAD_DIM = 64
N_GROUPS = 4
EXPERTS_PER_GROUP = 8
N_EXPERTS = N_GROUPS * EXPERTS_PER_GROUP

LANES = 128
INFO_EXPERT, INFO_GATE, INFO_RANK = 0, 2, 4
VMEM_LIMIT = 56 * 1024 * 1024

INPROJ_TM = 512
INPROJ_TN = 1024
SB_TQ = 512
SB_TK = 256
DIFF_T = 512
S5_CHUNK = 32
OUT_TM = 256
MOE_TM = 256
FINAL_TM = 256


def _params(*sem):
    return pltpu.CompilerParams(dimension_semantics=sem, vmem_limit_bytes=VMEM_LIMIT)


def _dot(a, b):
    return jnp.dot(a, b, preferred_element_type=F32)


def _dot_nt(a, b):
    return lax.dot_general(a, b, (((1,), (1,)), ((), ())), preferred_element_type=F32)


def _slab_rows(d):
    return d // LANES


def _load_slabs(ref, n_tok, d):
    s_rows = _slab_rows(d)
    return jnp.concatenate([ref[pl.ds(s, n_tok, stride=s_rows), :] for s in range(s_rows)], axis=1)


def _store_slabs(ref, x):
    n_tok, d = x.shape
    s_rows = _slab_rows(d)
    for s in range(s_rows):
        ref[pl.ds(s, n_tok, stride=s_rows), :] = x[:, s * LANES:(s + 1) * LANES]


def _split_bf16(x):
    hi = x.astype(BF16)
    lo = (x - hi.astype(F32)).astype(BF16)
    return hi, lo


def _moe_combine(h_ref, y0_ref, y1_ref, info_ref):
    info = info_ref[...]
    n_tok, d = h_ref.shape
    return (h_ref[...] + info[:, INFO_GATE:INFO_GATE + 1] * _load_slabs(y0_ref, n_tok, d)
            + info[:, INFO_GATE + 1:INFO_GATE + 2] * _load_slabs(y1_ref, n_tok, d))


def _inproj_body(*refs, after_moe):
    n_in = 4 if after_moe else 1
    g_ref, w_ref = refs[n_in:n_in + 2]
    if after_moe:
        proj_ref, x_out_ref, xn_ref = refs[n_in + 2:]
    else:
        proj_ref, xn_ref = refs[n_in + 2:]

    @pl.when(pl.program_id(1) == 0)
    def _():
        if after_moe:
            x = _moe_combine(*refs[:4])
            x_out_ref[...] = x
        else:
            x = refs[0][...]
        ms = jnp.mean(x * x, axis=-1, keepdims=True)
        xn_ref[...] = (x * lax.rsqrt(ms + EPS) * g_ref[...]).astype(BF16)

    proj_ref[...] = _dot(xn_ref[...], w_ref[...]).astype(BF16)


def _norm_inproj(x, moe_out, g, w_all, layer):
    t, d = x.shape
    n = w_all.shape[-1]
    tm, tn = min(INPROJ_TM, t), INPROJ_TN
    rows = lambda blk, w: pl.BlockSpec((tm, w), lambda i, j: (i + blk, 0))
    args, in_specs = [x], [rows(0, d)]
    if moe_out is not None:
        y, info = moe_out
        slabs = lambda blk: pl.BlockSpec((tm * _slab_rows(d), LANES), lambda i, j: (i + blk, 0))
        args += [y, y, info]
        in_specs += [slabs(0), slabs(t // tm), rows(0, LANES)]
    in_specs += [pl.BlockSpec((1, d), lambda i, j: (0, 0)),
                 pl.BlockSpec((None, d, tn), lambda i, j: (layer, 0, j))]
    out_shape = [jax.ShapeDtypeStruct((t, n), BF16)]
    out_specs = [pl.BlockSpec((tm, tn), lambda i, j: (i, j))]
    if moe_out is not None:
        out_shape.append(jax.ShapeDtypeStruct((t, d), F32))
        out_specs.append(pl.BlockSpec((tm, d), lambda i, j: (i, 0)))
    outs = pl.pallas_call(
        functools.partial(_inproj_body, after_moe=moe_out is not None),
        grid=(t // tm, n // tn),
        in_specs=in_specs, out_specs=out_specs, out_shape=out_shape,
        scratch_shapes=[pltpu.VMEM((tm, d), BF16)],
        compiler_params=_params("arbitrary", "arbitrary"),
        name="norm_inproj",
    )(*args, g, w_all)
    return outs if moe_out is not None else (outs[0], x)


def _sb_body(q_ref, k_ref, v_ref, o_ref, *, tq, tk):
    qi = pl.program_id(2)
    hd = SB_HEAD_DIM
    nsub = tq // tk
    lane = lax.broadcasted_iota(jnp.int32, (1, 2 * hd), 1)
    head_lanes = (lane < hd, lane >= hd)
    qs = q_ref[...] * (hd ** -0.5)
    zero = jnp.zeros((), BF16)
    qm = [jnp.where(m, qs, zero) for m in head_lanes]
    later = (lax.broadcasted_iota(jnp.int32, (tk, tk), 0)
             > lax.broadcasted_iota(jnp.int32, (tk, tk), 1)).astype(BF16)
    q_pos = qi * tq + lax.broadcasted_iota(jnp.int32, (tq, tk), 0)
    k_off = lax.broadcasted_iota(jnp.int32, (tq, tk), 1)

    def block(kb, carry, diag):
        acc, runs = carry
        start = pl.multiple_of(kb * tk, tk)
        kblk = k_ref[pl.ds(start, tk), :]
        vblk = v_ref[pl.ds(start, tk), :]
        if diag:
            strict = kb * tk + k_off < q_pos
        new_runs = []
        for h in range(2):
            z = _dot_nt(qm[h], kblk)
            log_fail = -jnp.maximum(z, 0.0) - jnp.log(1.0 + jnp.exp(-jnp.abs(z)))
            log_hit = log_fail + z
            if diag:
                log_fail = jnp.where(strict, log_fail, 0.0)
            log_after = _dot(log_fail.astype(BF16), later)
            w = jnp.exp(log_hit + log_after + runs[h])
            if diag:
                w = jnp.where(strict, w, 0.0)
            vm = jnp.where(head_lanes[h], vblk, zero)
            acc = acc + _dot(w.astype(BF16), vm)
            new_runs.append(runs[h] + jnp.sum(log_fail, axis=-1, keepdims=True))
        return acc, tuple(new_runs)

    zrun = jnp.zeros((tq, 1), F32)
    carry = (jnp.zeros((tq, 2 * hd), F32), (zrun, zrun))
    for sub in reversed(range(nsub)):
        carry = block(nsub * qi + sub, carry, True)
    carry = lax.fori_loop(0, nsub * qi, lambda it, c: block(nsub * qi - 1 - it, c, False), carry)
    o_ref[...] = carry[0].astype(BF16)


def _sb_attention(proj, bsz, seq, width):
    tq, tk = min(SB_TQ, seq), min(SB_TK, seq)
    nq = seq // tq
    npair = width // LANES
    kernel = functools.partial(_sb_body, tq=tq, tk=tk)
    return pl.pallas_call(
        kernel,
        grid=(bsz, npair, nq),
        in_specs=[pl.BlockSpec((tq, LANES), lambda b, p, i: (b * nq + i, p)),
                  pl.BlockSpec((seq, LANES), lambda b, p, i: (b, npair + p)),
                  pl.BlockSpec((seq, LANES), lambda b, p, i: (b, 2 * npair + p))],
        out_specs=pl.BlockSpec((tq, LANES), lambda b, p, i: (b * nq + i, p)),
        out_shape=jax.ShapeDtypeStruct((bsz * seq, width), BF16),
        compiler_params=_params("arbitrary", "arbitrary", "arbitrary"),
        name="sb_attention",
    )(proj, proj, proj)


def _diff_body(lq1_ref, lk1_ref, lq2_ref, lk2_ref, g_ref, q_ref, k_ref, v_ref, o_ref, s_scr, *, t, lam_init):
    qi = pl.program_id(2)
    hd = DIFF_HEAD_DIM
    lam = (jnp.exp(jnp.sum(lq1_ref[...] * lk1_ref[...], axis=-1, keepdims=True))
           - jnp.exp(jnp.sum(lq2_ref[...] * lk2_ref[...], axis=-1, keepdims=True)) + lam_init)
    lane = lax.broadcasted_iota(jnp.int32, (1, 2 * hd), 1)
    qs = q_ref[...] * (hd ** -0.5)
    zero = jnp.zeros((), BF16)
    qm = [jnp.where(lane < hd, qs, zero), jnp.where(lane >= hd, qs, zero)]
    row = lax.broadcasted_iota(jnp.int32, (t, t), 0)
    col = lax.broadcasted_iota(jnp.int32, (t, t), 1)
    causal = col <= row

    def score_block(kb, mx, diag):
        kblk = k_ref[pl.ds(pl.multiple_of(kb * t, t), t), :]
        out = []
        for h in range(2):
            s = _dot_nt(qm[h], kblk)
            if diag:
                s = jnp.where(causal, s, -jnp.inf)
            s_scr[h, kb] = s
            smax = mx[h]
            for c in range(t // LANES):
                smax = jnp.maximum(smax, s[:, c * LANES:(c + 1) * LANES])
            out.append(smax)
        return tuple(out)

    mx0 = jnp.full((t, LANES), -jnp.inf, F32)
    mx = score_block(qi, (mx0, mx0), True)
    mx = lax.fori_loop(0, qi, lambda kb, c: score_block(kb, c, False), mx)
    m = [jnp.broadcast_to(jnp.max(mx[h], axis=-1, keepdims=True), (t, t)) for h in range(2)]
    ones = jnp.ones((t, LANES), BF16)

    def pv_block(kb, acc):
        vaug = jnp.concatenate([v_ref[pl.ds(pl.multiple_of(kb * t, t), t), :], ones], axis=1)
        return tuple(acc[h] + _dot(jnp.exp(s_scr[h, kb] - m[h]).astype(BF16), vaug) for h in range(2))

    acc0 = jnp.zeros((t, 2 * LANES), F32)
    a0, a1 = lax.fori_loop(0, qi + 1, pv_block, (acc0, acc0))
    o = a0[:, :LANES] / a0[:, LANES:LANES + 1] - lam * (a1[:, :LANES] / a1[:, LANES:LANES + 1])
    ms = jnp.mean(o * o, axis=-1, keepdims=True)
    o_ref[...] = (o * lax.rsqrt(ms + EPS) * g_ref[...] * (1.0 - lam_init)).astype(BF16)


def _diff_attention(proj, lam_params, g, bsz, seq, col0, width, lam_init):
    t = min(DIFF_T, seq)
    nq = seq // t
    nh = width // LANES
    c0 = col0 // LANES
    small = pl.BlockSpec((1, DIFF_HEAD_DIM), lambda b, h, i: (0, 0))
    kernel = functools.partial(_diff_body, t=t, lam_init=lam_init)
    return pl.pallas_call(
        kernel,
        grid=(bsz, nh, nq),
        in_specs=[small, small, small, small,
                  pl.BlockSpec((1, LANES), lambda b, h, i: (0, 0)),
                  pl.BlockSpec((t, LANES), lambda b, h, i: (b * nq + i, c0 + h)),
                  pl.BlockSpec((seq, LANES), lambda b, h, i: (b, c0 + nh + h)),
                  pl.BlockSpec((seq, LANES), lambda b, h, i: (b, c0 + 2 * nh + h))],
        out_specs=pl.BlockSpec((t, LANES), lambda b, h, i: (b * nq + i, h)),
        out_shape=jax.ShapeDtypeStruct((bsz * seq, width), BF16),
        scratch_shapes=[pltpu.VMEM((2, nq, t, t), F32)],
        compiler_params=_params("arbitrary", "arbitrary", "arbitrary"),
        name="diff_attention",
    )(*lam_params, g, proj, proj, proj)


def _s5_body(u_ref, m_ref, ere_ref, eim_ref, fre_ref, fim_ref, are_ref, aim_ref, y_ref,
             xre_s, xim_s, sre_s, sim_s, *, bsz, nchunk):
    u = u_ref[...]
    xre_s[...] = _dot(u, ere_ref[...])
    xim_s[...] = _dot(u, eim_ref[...])
    are = are_ref[...]
    aim = aim_ref[...]
    sre = jnp.zeros((bsz, SSM_STATE), F32)
    sim = jnp.zeros((bsz, SSM_STATE), F32)
    for j in range(nchunk):
        rows = pl.ds(j * bsz, bsz)
        sre_s[rows, :] = sre
        sim_s[rows, :] = sim
        sre, sim = (are * sre - aim * sim + xre_s[rows, :],
                    are * sim + aim * sre + xim_s[rows, :])
    y_ref[...] = (_dot(u, m_ref[...])
                  + _dot(sre_s[...].astype(BF16), fre_ref[...])
                  + _dot(sim_s[...].astype(BF16), fim_ref[...]))


def _s5_operators(lam_re, lam_im, b_re, b_im, c_re, c_im, d_skip, log_dt, chunk):
    hp = lax.Precision.HIGHEST
    g, n = lam_re.shape
    ch = b_re.shape[-1]
    dt = jnp.exp(log_dt.astype(F32))[:, None]
    k = jnp.arange(chunk + 1, dtype=F32)[None, :, None]
    mag = jnp.exp(k * (lam_re * dt)[:, None, :])
    ang = k * (lam_im * dt)[:, None, :]
    p_re, p_im = mag * jnp.cos(ang), mag * jnp.sin(ang)
    lb_re, lb_im = p_re[:, 1], p_im[:, 1]
    den = lam_re * lam_re + lam_im * lam_im
    q_re = ((lb_re - 1.0) * lam_re + lb_im * lam_im) / den
    q_im = (lb_im * lam_re - (lb_re - 1.0) * lam_im) / den
    bb_re = q_re[:, :, None] * b_re - q_im[:, :, None] * b_im
    bb_im = q_re[:, :, None] * b_im + q_im[:, :, None] * b_re
    pb_re = p_re[:, :, :, None] * bb_re[:, None] - p_im[:, :, :, None] * bb_im[:, None]
    pb_im = p_re[:, :, :, None] * bb_im[:, None] + p_im[:, :, :, None] * bb_re[:, None]
    w = (jnp.einsum('gon,gkni->gkio', c_re, pb_re[:, :chunk], precision=hp)
         - jnp.einsum('gon,gkni->gkio', c_im, pb_im[:, :chunk], precision=hp))
    w = w.at[:, 0].add(jnp.eye(ch, dtype=F32)[None] * d_skip[:, :, None])
    s_idx = jnp.arange(chunk)[:, None]
    t_idx = jnp.arange(chunk)[None, :]
    lag = t_idx - s_idx
    m5 = jnp.where((lag >= 0)[None, :, :, None, None], w[:, jnp.clip(lag, 0, chunk - 1)], 0.0)
    m = m5.transpose(0, 1, 3, 2, 4).reshape(g, chunk * ch, chunk * ch)
    e_re = pb_re[:, :chunk][:, ::-1].transpose(0, 1, 3, 2).reshape(g, chunk * ch, n)
    e_im = pb_im[:, :chunk][:, ::-1].transpose(0, 1, 3, 2).reshape(g, chunk * ch, n)
    cp_re = c_re[:, None] * p_re[:, 1:, None, :] - c_im[:, None] * p_im[:, 1:, None, :]
    cp_im = c_re[:, None] * p_im[:, 1:, None, :] + c_im[:, None] * p_re[:, 1:, None, :]
    f_re = cp_re.transpose(0, 3, 1, 2).reshape(g, n, chunk * ch)
    f_im = (-cp_im).transpose(0, 3, 1, 2).reshape(g, n, chunk * ch)
    a_re = p_re[:, chunk][:, None, :]
    a_im = p_im[:, chunk][:, None, :]
    return (m.astype(BF16), e_re.astype(BF16), e_im.astype(BF16), f_re.astype(BF16), f_im.astype(BF16),
            a_re, a_im)


def _s5_scan(proj, ops, bsz, seq, col0, width):
    m, e_re, e_im, f_re, f_im, a_re, a_im = ops
    g = m.shape[0]
    ch = width // g
    chunk = m.shape[1] // ch
    nchunk = seq // chunk
    rows, cw, n = nchunk * bsz, chunk * ch, SSM_STATE
    u = proj[:, col0:col0 + width].reshape(bsz, nchunk, chunk, g, ch)
    u = u.transpose(3, 1, 0, 2, 4).reshape(g, rows, cw)
    grp = lambda *shape: pl.BlockSpec((None,) + shape, lambda i: (i, 0, 0))
    y = pl.pallas_call(
        functools.partial(_s5_body, bsz=bsz, nchunk=nchunk),
        grid=(g,),
        in_specs=[grp(rows, cw), grp(cw, cw), grp(cw, n), grp(cw, n), grp(n, cw), grp(n, cw),
                  grp(1, n), grp(1, n)],
        out_specs=grp(rows, cw),
        out_shape=jax.ShapeDtypeStruct((g, rows, cw), F32),
        scratch_shapes=[pltpu.VMEM((rows, n), F32)] * 4,
        compiler_params=_params("arbitrary"),
        name="s5_scan",
    )(u, m, e_re, e_im, f_re, f_im, a_re, a_im)
    y = y.reshape(g, nchunk, bsz, chunk, ch).transpose(2, 1, 3, 0, 4)
    return y.reshape(bsz * seq, width)


def _rms(x, g):
    ms = jnp.mean(x * x, axis=-1, keepdims=True)
    return x * lax.rsqrt(ms + EPS) * g


def _outproj_body(ysb_ref, yssm_ref, ydf_ref, x_ref, sbg_ref, wglu_ref, bglu_ref, ssmg_ref, wout_ref,
                  ffng_ref, wrhi_ref, wrlo_ref, rb_ref,
                  h_ref, hn_ref, info_ref, cnt_ref, run_s, *, tm, sbw, ssw):
    step = pl.program_id(0)

    @pl.when(step == 0)
    def _():
        run_s[...] = jnp.zeros_like(run_s)

    ysb = _rms(ysb_ref[...].astype(F32), sbg_ref[...]).astype(BF16)
    y = jax.nn.gelu(yssm_ref[...])
    y = y * jax.nn.sigmoid(_dot(y.astype(BF16), wglu_ref[...]) + bglu_ref[...])
    yssm = _rms(y, ssmg_ref[...]).astype(BF16)
    h = (x_ref[...]
         + _dot(ysb, wout_ref[0:sbw, :])
         + _dot(yssm, wout_ref[sbw:sbw + ssw, :])
         + _dot(ydf_ref[...], wout_ref[sbw + ssw:, :]))
    h_ref[...] = h
    hn = _rms(h, ffng_ref[...])
    _store_slabs(hn_ref, hn)

    hi, lo = _split_bf16(hn)
    logits = (_dot(hi, wrhi_ref[...]) + _dot(hi, wrlo_ref[...]) + _dot(lo, wrhi_ref[...])) + rb_ref[...]
    lane = lax.broadcasted_iota(jnp.int32, (tm, LANES), 1).astype(F32)
    ninf = -jnp.inf

    def first_max(v):
        m = jnp.max(v, axis=-1, keepdims=True)
        idx = jnp.min(jnp.where(v == m, lane, float(LANES)), axis=-1, keepdims=True)
        return m, idx

    gl = jnp.where(lane < N_GROUPS, logits, ninf)
    gmax, gidx = first_max(gl)
    g_top = 1.0 / jnp.sum(jnp.exp(gl - gmax), axis=-1, keepdims=True)
    group_lo = N_GROUPS + EXPERTS_PER_GROUP * gidx
    in_group = (lane >= group_lo) & (lane < group_lo + EXPERTS_PER_GROUP)
    el = jnp.where(in_group, logits, ninf)
    m1, i1 = first_max(el)
    m2, i2 = first_max(jnp.where(lane == i1, ninf, el))
    r = jnp.exp(m2 - m1)
    w_a = g_top / (1.0 + r)
    w_b = g_top * r / (1.0 + r)
    e_a = i1 - N_GROUPS
    e_b = i2 - N_GROUPS

    oh_a = (lane == e_a).astype(F32)
    oh_b = (lane == e_b).astype(F32)
    cnt = oh_a + oh_b
    trow = lax.broadcasted_iota(jnp.int32, (tm, tm), 0)
    tcol = lax.broadcasted_iota(jnp.int32, (tm, tm), 1)
    before = (tcol < trow).astype(BF16)
    base = _dot(before, cnt.astype(BF16)) + run_s[...]
    rank_a = jnp.sum(oh_a * base, axis=-1, keepdims=True)
    rank_b = jnp.sum(oh_b * base, axis=-1, keepdims=True)
    run_s[...] = run_s[...] + jnp.sum(cnt, axis=0, keepdims=True)
    cnt_ref[...] = run_s[...]

    info = jnp.zeros((tm, LANES), F32)
    for k, val in ((INFO_EXPERT, e_a), (INFO_EXPERT + 1, e_b), (INFO_GATE, w_a), (INFO_GATE + 1, w_b),
                   (INFO_RANK, rank_a), (INFO_RANK + 1, rank_b)):
        info = jnp.where(lane == k, val, info)
    info_ref[...] = info


def _outproj_router(ysb, yssm, ydf, x, sbg, wglu, bglu, ssmg, wout_all, ffng, wr_hi, wr_lo, rb, layer):
    t, d = x.shape
    tm = min(OUT_TM, t)
    sbw, ssw, dfw = ysb.shape[1], yssm.shape[1], ydf.shape[1]
    rowblk = lambda w: pl.BlockSpec((tm, w), lambda i: (i, 0))
    const = lambda *shape: pl.BlockSpec(shape, lambda i: (0,) * len(shape))
    kernel = functools.partial(_outproj_body, tm=tm, sbw=sbw, ssw=ssw)
    return pl.pallas_call(
        kernel,
        grid=(t // tm,),
        in_specs=[rowblk(sbw), rowblk(ssw), rowblk(dfw), rowblk(d),
                  const(1, sbw), const(ssw, ssw), const(1, ssw), const(1, ssw),
                  pl.BlockSpec((None, d, d), lambda i: (layer, 0, 0)),
                  const(1, d), const(d, LANES), const(d, LANES), const(1, LANES)],
        out_specs=[rowblk(d), pl.BlockSpec((tm * _slab_rows(d), LANES), lambda i: (i, 0)),
                   rowblk(LANES), const(1, LANES)],
        out_shape=[jax.ShapeDtypeStruct((t, d), F32), jax.ShapeDtypeStruct((t * _slab_rows(d), LANES), F32),
                   jax.ShapeDtypeStruct((t, LANES), F32), jax.ShapeDtypeStruct((1, LANES), F32)],
        scratch_shapes=[pltpu.VMEM((1, LANES), F32)],
        compiler_params=_params("arbitrary"),
        name="outproj_router",
    )(ysb, yssm, ydf, x, sbg, wglu, bglu, ssmg, wout_all, ffng, wr_hi, wr_lo, rb)


def _moe_body(te_ref, nu_ref, src_ref, dst_ref,
              hn_hbm, w1_ref, w3_ref, w2_ref,
              y_hbm,
              xbuf0, xbuf1, ybuf0, ybuf1, w1b, w3b, w2b, gsem, ssem, *, tm, d):
    i = pl.program_id(0)
    n_used = nu_ref[0]
    xbufs = (xbuf0, xbuf1)
    ybufs = (ybuf0, ybuf1)
    s_rows = _slab_rows(d)

    def slab(ref, tok):
        return ref.at[pl.ds(pl.multiple_of(tok * s_rows, s_rows), s_rows), :]

    def gather_rows(tile, par, start):
        base = tile * tm
        for r in range(tm):
            cp = pltpu.make_async_copy(slab(hn_hbm, src_ref[base + r]), slab(xbufs[par], r), gsem.at[par])
            cp.start() if start else cp.wait()

    def scatter_rows(tile, par, start):
        base = (tile + 1) * tm
        for r in range(tm):
            cp = pltpu.make_async_copy(slab(ybufs[par], r), slab(y_hbm, dst_ref[base + r]), ssem.at[par])
            cp.start() if start else cp.wait()

    @pl.when(i == 0)
    def _():
        ybuf1[...] = jnp.zeros_like(ybuf1)
        gather_rows(0, 0, True)
        gather_rows(0, 0, False)

    new_expert = jnp.logical_or(i == 0, te_ref[i] != te_ref[jnp.maximum(i - 1, 0)])

    @pl.when(jnp.logical_and(i < n_used, new_expert))
    def _():
        w1b[...] = w1_ref[...].astype(BF16)
        w3b[...] = w3_ref[...].astype(BF16)
        w2b[...] = w2_ref[...].astype(BF16)

    for par in range(2):
        active = jnp.logical_and(i < n_used, i % 2 == par)
        nxt = jnp.minimum(i + 1, n_used - 1)

        @pl.when(active)
        def _(par=par, nxt=nxt):
            x = _load_slabs(xbufs[par], tm, d).astype(BF16)
            gather_rows(nxt, 1 - par, True)
            scatter_rows(i - 1, 1 - par, True)
            h1 = _dot(x, w1b[...])
            h3 = _dot(x, w3b[...])
            a = (h1 * jax.nn.sigmoid(h1) * h3).astype(BF16)
            _store_slabs(ybufs[par], _dot(a, w2b[...]))

        @pl.when(active)
        def _(par=par, nxt=nxt):
            scatter_rows(i - 1, 1 - par, False)
            gather_rows(nxt, 1 - par, False)

        @pl.when(jnp.logical_and(active, i == n_used - 1))
        def _(par=par):
            scatter_rows(i, par, True)
            scatter_rows(i, par, False)


def _moe_experts(hn, tile_expert, n_used, src_tok, dst_row, w1_all, w3_all, w2_all, layer):
    d = w1_all.shape[-2]
    s_rows = _slab_rows(d)
    t = hn.shape[0] // s_rows
    tm = MOE_TM
    n_tiles = tile_expert.shape[0]
    f = w1_all.shape[-1]
    wspec = lambda a, b: pl.BlockSpec((None, None, a, b), lambda i, te, nu, src, dst: (layer, te[i], 0, 0))
    grid_spec = pltpu.PrefetchScalarGridSpec(
        num_scalar_prefetch=4,
        grid=(n_tiles,),
        in_specs=[pl.BlockSpec(memory_space=pl.ANY), wspec(d, f), wspec(d, f), wspec(f, d)],
        out_specs=pl.BlockSpec(memory_space=pl.ANY),
        scratch_shapes=[pltpu.VMEM((tm * s_rows, LANES), F32)] * 4
                       + [pltpu.VMEM((d, f), BF16), pltpu.VMEM((d, f), BF16), pltpu.VMEM((f, d), BF16),
                          pltpu.SemaphoreType.DMA((2,)), pltpu.SemaphoreType.DMA((2,))],
    )
    return pl.pallas_call(
        functools.partial(_moe_body, tm=tm, d=d),
        grid_spec=grid_spec,
        out_shape=jax.ShapeDtypeStruct(((2 * t + tm) * s_rows, LANES), F32),
        compiler_params=_params("arbitrary"),
        name="moe_experts",
    )(tile_expert, n_used, src_tok, dst_row, hn, w1_all, w3_all, w2_all)


def _moe_schedule(info, counts, t):
    tm = MOE_TM
    n_tiles = (2 * t) // tm + N_EXPERTS
    p = n_tiles * tm
    e_id = info[:, INFO_EXPERT:INFO_EXPERT + 2].astype(jnp.int32)
    rank = info[:, INFO_RANK:INFO_RANK + 2].astype(jnp.int32)
    cnt = counts[0, :N_EXPERTS].astype(jnp.int32)
    tiles_e = (cnt + tm - 1) // tm
    tile_end = jnp.cumsum(tiles_e)
    tile_start = tile_end - tiles_e
    n_used = tile_end[-1]
    pos = (tile_start * tm)[e_id] + rank
    tok = jnp.broadcast_to(jnp.arange(t, dtype=jnp.int32)[:, None], (t, 2))
    dst = tok + jnp.array([0, t], jnp.int32)[None, :]
    trash = 2 * t + jnp.arange(p, dtype=jnp.int32) % tm
    default = jnp.stack([jnp.zeros((p,), jnp.int32), trash], axis=1)
    update = jnp.stack([tok.reshape(-1), dst.reshape(-1)], axis=1)
    table = default.at[pos.reshape(-1)].set(update)
    src_tok = table[:, 0]
    dst_row = jnp.concatenate([trash[:tm], table[:, 1]])
    tile_ids = jnp.minimum(jnp.arange(n_tiles, dtype=jnp.int32), n_used - 1)
    tile_expert = jnp.sum(tile_ids[:, None] >= tile_end[None, :], axis=1).astype(jnp.int32)
    return tile_expert, n_used.reshape(1).astype(jnp.int32), src_tok, dst_row


def _final_body(h_ref, y0_ref, y1_ref, info_ref, g_ref, o_ref):
    o_ref[...] = _rms(_moe_combine(h_ref, y0_ref, y1_ref, info_ref), g_ref[...])


def _final_norm(h, y, info, g):
    t, d = h.shape
    tm = min(FINAL_TM, t)
    nblk = t // tm
    return pl.pallas_call(
        _final_body,
        grid=(nblk,),
        in_specs=[pl.BlockSpec((tm, d), lambda i: (i, 0)),
                  pl.BlockSpec((tm * _slab_rows(d), LANES), lambda i: (i, 0)),
                  pl.BlockSpec((tm * _slab_rows(d), LANES), lambda i: (i + nblk, 0)),
                  pl.BlockSpec((tm, LANES), lambda i: (i, 0)),
                  pl.BlockSpec((1, d), lambda i: (0, 0))],
        out_specs=pl.BlockSpec((tm, d), lambda i: (i, 0)),
        out_shape=jax.ShapeDtypeStruct((t, d), F32),
        compiler_params=_params("arbitrary"),
        name="final_norm",
    )(h, y, y, info, g)


def kernel(x, norm_mix_g, w_in, sb_norm_g, ssm_lam_re, ssm_lam_im, ssm_b_re, ssm_b_im, ssm_c_re, ssm_c_im, ssm_d, ssm_log_dt, ssm_w_glu, ssm_b_glu, ssm_norm_g, diff_lq1, diff_lk1, diff_lq2, diff_lk2, diff_subln_g, w_out, norm_ffn_g, router_group_w, router_group_b, router_expert_w, router_expert_b, expert_w1, expert_w3, expert_w2, final_norm_g):
    bsz, seq, d = x.shape
    depth = w_in.shape[0]
    t = bsz * seq
    sbw = sb_norm_g.shape[-1]
    ssw = ssm_norm_g.shape[-1]
    dfw = d - sbw - ssw
    ssm_col = 3 * sbw
    diff_col = ssm_col + ssw

    w_in_b = w_in.astype(BF16)
    w_out_b = w_out.astype(BF16)
    w_glu_b = ssm_w_glu.astype(BF16)
    row = lambda v: v.reshape(1, -1).astype(F32)

    h = x.reshape(t, d)
    moe_out = None
    for l in range(depth):
        lam_init = 0.8 - 0.6 * math.exp(-0.3 * l)
        proj, xres = _norm_inproj(h, moe_out, row(norm_mix_g[l]), w_in_b, l)
        ysb = _sb_attention(proj, bsz, seq, sbw)
        ydf = _diff_attention(proj, [row(p[l]) for p in (diff_lq1, diff_lk1, diff_lq2, diff_lk2)],
                              row(diff_subln_g[l]), bsz, seq, diff_col, dfw, lam_init)
        ops = _s5_operators(ssm_lam_re[l], ssm_lam_im[l], ssm_b_re[l], ssm_b_im[l], ssm_c_re[l], ssm_c_im[l],
                            ssm_d[l], ssm_log_dt[l], S5_CHUNK)
        yssm = _s5_scan(proj, ops, bsz, seq, ssm_col, ssw)

        wr = jnp.zeros((d, LANES), F32)
        wr = wr.at[:, :N_GROUPS].set(router_group_w[l])
        wr = wr.at[:, N_GROUPS:N_GROUPS + N_EXPERTS].set(
            router_expert_w[l].transpose(1, 0, 2).reshape(d, N_EXPERTS))
        wr_hi, wr_lo = _split_bf16(wr)
        rb = jnp.zeros((1, LANES), F32)
        rb = rb.at[0, :N_GROUPS].set(router_group_b[l])
        rb = rb.at[0, N_GROUPS:N_GROUPS + N_EXPERTS].set(router_expert_b[l].reshape(-1))

        h, hn, info, counts = _outproj_router(
            ysb, yssm, ydf, xres, row(sb_norm_g[l]), w_glu_b[l], row(ssm_b_glu[l]), row(ssm_norm_g[l]),
            w_out_b, row(norm_ffn_g[l]), wr_hi, wr_lo, rb, l)
        sched = _moe_schedule(info, counts, t)
        moe_out = (_moe_experts(hn, *sched, expert_w1, expert_w3, expert_w2, l), info)
    out = _final_norm(h, *moe_out, row(final_norm_g))
    return out.reshape(bsz, seq, d)
```

```python
import functools
import math

import jax
import jax.numpy as jnp
from jax import lax
from jax.experimental import pallas as pl
from jax.experimental.pallas import tpu as pltpu

F32 = jnp.float32
BF16 = jnp.bfloat16

EPS = 1e-6
SB_HEAD_DIM = 64
SSM_GROUP_CH = 16
SSM_STATE = 64
DIFF_HEAD_DIM = 64
N_GROUPS = 4
EXPERTS_PER_GROUP = 8
N_EXPERTS = N_GROUPS * EXPERTS_PER_GROUP

LANES = 128
INFO_EXPERT, INFO_GATE, INFO_RANK = 0, 2, 4
VMEM_LIMIT = 56 * 1024 * 1024

INPROJ_TM = 512
INPROJ_TN = 1024
SB_TQ = 512
SB_TK = 256
DIFF_T = 512
S5_CHUNK = 8
OUT_TM = 256
MOE_TM = 256
FINAL_TM = 256


def _params(*sem):
    return pltpu.CompilerParams(dimension_semantics=sem, vmem_limit_bytes=VMEM_LIMIT)


def _dot(a, b):
    return jnp.dot(a, b, preferred_element_type=F32)


def _dot_nt(a, b):
    return lax.dot_general(a, b, (((1,), (1,)), ((), ())), preferred_element_type=F32)


def _slab_rows(d):
    return d // LANES


def _load_slabs(ref, n_tok, d):
    s_rows = _slab_rows(d)
    return jnp.concatenate([ref[pl.ds(s, n_tok, stride=s_rows), :] for s in range(s_rows)], axis=1)


def _store_slabs(ref, x):
    n_tok, d = x.shape
    s_rows = _slab_rows(d)
    for s in range(s_rows):
        ref[pl.ds(s, n_tok, stride=s_rows), :] = x[:, s * LANES:(s + 1) * LANES]


def _load_wide(ref, rows=slice(None)):
    return jnp.concatenate([ref[c, rows, :] for c in range(ref.shape[0])], axis=1)


def _store_wide(ref, x, rows=slice(None)):
    for c in range(ref.shape[0]):
        ref[c, rows, :] = x[:, c * LANES:(c + 1) * LANES]


def _split_bf16(x):
    hi = x.astype(BF16)
    lo = (x - hi.astype(F32)).astype(BF16)
    return hi, lo


def _moe_combine(h_ref, y0_ref, y1_ref, info_ref):
    info = info_ref[...]
    n_tok, d = h_ref.shape
    return (h_ref[...] + info[:, INFO_GATE:INFO_GATE + 1] * _load_slabs(y0_ref, n_tok, d)
            + info[:, INFO_GATE + 1:INFO_GATE + 2] * _load_slabs(y1_ref, n_tok, d))


def _inproj_body(*refs, after_moe, ssm_blk, ssm_off, ssm_w, chunk):
    n_in = 4 if after_moe else 1
    g_ref, w_ref = refs[n_in:n_in + 2]
    if after_moe:
        proj_ref, ussm_ref, x_out_ref, xn_ref, u_scr = refs[n_in + 2:]
    else:
        proj_ref, ussm_ref, xn_ref, u_scr = refs[n_in + 2:]

    @pl.when(pl.program_id(1) == 0)
    def _():
        if after_moe:
            x = _moe_combine(*refs[:4])
            x_out_ref[...] = x
        else:
            x = refs[0][...]
        ms = jnp.mean(x * x, axis=-1, keepdims=True)
        xn_ref[...] = (x * lax.rsqrt(ms + EPS) * g_ref[...]).astype(BF16)

    res = _dot(xn_ref[...], w_ref[...])
    proj_ref[...] = res.astype(BF16)

    @pl.when(pl.program_id(1) == ssm_blk)
    def _():
        _store_wide(u_scr, res[:, ssm_off:ssm_off + ssm_w])
        n_chunks = u_scr.shape[1] // chunk
        for t in range(chunk):
            ussm_ref[t] = _load_wide(u_scr, pl.ds(t, n_chunks, stride=chunk)).astype(BF16)


def _norm_inproj(x, moe_out, g, w_all, layer, ssm_col, ssm_w):
    t, d = x.shape
    n = w_all.shape[-1]
    tm, tn = min(INPROJ_TM, t), INPROJ_TN
    chunk = S5_CHUNK
    ssm_blk, ssm_off = divmod(ssm_col, tn)
    assert ssm_off + ssm_w <= tn and tm % chunk == 0
    rows = lambda blk, w: pl.BlockSpec((tm, w), lambda i, j: (i + blk, 0))
    args, in_specs = [x], [rows(0, d)]
    if moe_out is not None:
        y, info = moe_out
        slabs = lambda blk: pl.BlockSpec((tm * _slab_rows(d), LANES), lambda i, j: (i + blk, 0))
        args += [y, y, info]
        in_specs += [slabs(0), slabs(t // tm), rows(0, LANES)]
    in_specs += [pl.BlockSpec((1, d), lambda i, j: (0, 0)),
                 pl.BlockSpec((None, d, tn), lambda i, j: (layer, 0, j))]
    out_shape = [jax.ShapeDtypeStruct((t, n), BF16), jax.ShapeDtypeStruct((chunk, t // chunk, ssm_w), BF16)]
    out_specs = [pl.BlockSpec((tm, tn), lambda i, j: (i, j)),
                 pl.BlockSpec((chunk, tm // chunk, ssm_w), lambda i, j: (0, i, 0))]
    if moe_out is not None:
        out_shape.append(jax.ShapeDtypeStruct((t, d), F32))
        out_specs.append(pl.BlockSpec((tm, d), lambda i, j: (i, 0)))
    outs = pl.pallas_call(
        functools.partial(_inproj_body, after_moe=moe_out is not None, ssm_blk=ssm_blk, ssm_off=ssm_off,
                          ssm_w=ssm_w, chunk=chunk),
        grid=(t // tm, n // tn),
        in_specs=in_specs, out_specs=out_specs, out_shape=out_shape,
        scratch_shapes=[pltpu.VMEM((tm, d), BF16), pltpu.VMEM((ssm_w // LANES, tm, LANES), F32)],
        compiler_params=_params("arbitrary", "arbitrary"),
        name="norm_inproj",
    )(*args, g, w_all)
    return outs if moe_out is not None else (outs[0], outs[1], x)


def _sb_body(q_ref, k_ref, v_ref, o_ref, *, tq, tk):
    qi = pl.program_id(2)
    hd = SB_HEAD_DIM
    nsub = tq // tk
    lane = lax.broadcasted_iota(jnp.int32, (1, 2 * hd), 1)
    head_lanes = (lane < hd, lane >= hd)
    qs = q_ref[...] * (hd ** -0.5)
    zero = jnp.zeros((), BF16)
    qm = [jnp.where(m, qs, zero) for m in head_lanes]
    later = (lax.broadcasted_iota(jnp.int32, (tk, tk), 0)
             > lax.broadcasted_iota(jnp.int32, (tk, tk), 1)).astype(BF16)
    q_pos = qi * tq + lax.broadcasted_iota(jnp.int32, (tq, tk), 0)
    k_off = lax.broadcasted_iota(jnp.int32, (tq, tk), 1)

    def block(kb, carry, diag):
        acc, runs = carry
        start = pl.multiple_of(kb * tk, tk)
        kblk = k_ref[pl.ds(start, tk), :]
        vblk = v_ref[pl.ds(start, tk), :]
        if diag:
            strict = kb * tk + k_off < q_pos
        new_runs = []
        for h in range(2):
            z = _dot_nt(qm[h], kblk)
            log_fail = -jnp.maximum(z, 0.0) - jnp.log(1.0 + jnp.exp(-jnp.abs(z)))
            log_hit = log_fail + z
            if diag:
                log_fail = jnp.where(strict, log_fail, 0.0)
            log_after = _dot(log_fail.astype(BF16), later)
            w = jnp.exp(log_hit + log_after + runs[h])
            if diag:
                w = jnp.where(strict, w, 0.0)
            vm = jnp.where(head_lanes[h], vblk, zero)
            acc = acc + _dot(w.astype(BF16), vm)
            new_runs.append(runs[h] + jnp.sum(log_fail, axis=-1, keepdims=True))
        return acc, tuple(new_runs)

    zrun = jnp.zeros((tq, 1), F32)
    carry = (jnp.zeros((tq, 2 * hd), F32), (zrun, zrun))
    for sub in reversed(range(nsub)):
        carry = block(nsub * qi + sub, carry, True)
    carry = lax.fori_loop(0, nsub * qi, lambda it, c: block(nsub * qi - 1 - it, c, False), carry)
    o_ref[...] = carry[0].astype(BF16)


def _sb_attention(proj, bsz, seq, width):
    tq, tk = min(SB_TQ, seq), min(SB_TK, seq)
    nq = seq // tq
    npair = width // LANES
    kernel = functools.partial(_sb_body, tq=tq, tk=tk)
    return pl.pallas_call(
        kernel,
        grid=(bsz, npair, nq),
        in_specs=[pl.BlockSpec((tq, LANES), lambda b, p, i: (b * nq + i, p)),
                  pl.BlockSpec((seq, LANES), lambda b, p, i: (b, npair + p)),
                  pl.BlockSpec((seq, LANES), lambda b, p, i: (b, 2 * npair + p))],
        out_specs=pl.BlockSpec((tq, LANES), lambda b, p, i: (b * nq + i, p)),
        out_shape=jax.ShapeDtypeStruct((bsz * seq, width), BF16),
        compiler_params=_params("arbitrary", "arbitrary", "arbitrary"),
        name="sb_attention",
    )(proj, proj, proj)


def _diff_body(lq1_ref, lk1_ref, lq2_ref, lk2_ref, g_ref, q_ref, k_ref, v_ref, o_ref, s_scr, *, t, lam_init):
    qi = pl.program_id(2)
    hd = DIFF_HEAD_DIM
    lam = (jnp.exp(jnp.sum(lq1_ref[...] * lk1_ref[...], axis=-1, keepdims=True))
           - jnp.exp(jnp.sum(lq2_ref[...] * lk2_ref[...], axis=-1, keepdims=True)) + lam_init)
    lane = lax.broadcasted_iota(jnp.int32, (1, 2 * hd), 1)
    qs = q_ref[...] * (hd ** -0.5)
    zero = jnp.zeros((), BF16)
    qm = [jnp.where(lane < hd, qs, zero), jnp.where(lane >= hd, qs, zero)]
    row = lax.broadcasted_iota(jnp.int32, (t, t), 0)
    col = lax.broadcasted_iota(jnp.int32, (t, t), 1)
    causal = col <= row

    def score_block(kb, mx, diag):
        kblk = k_ref[pl.ds(pl.multiple_of(kb * t, t), t), :]
        out = []
        for h in range(2):
            s = _dot_nt(qm[h], kblk)
            if diag:
                s = jnp.where(causal, s, -jnp.inf)
            s_scr[h, kb] = s
            smax = mx[h]
            for c in range(t // LANES):
                smax = jnp.maximum(smax, s[:, c * LANES:(c + 1) * LANES])
            out.append(smax)
        return tuple(out)

    mx0 = jnp.full((t, LANES), -jnp.inf, F32)
    mx = score_block(qi, (mx0, mx0), True)
    mx = lax.fori_loop(0, qi, lambda kb, c: score_block(kb, c, False), mx)
    m = [jnp.broadcast_to(jnp.max(mx[h], axis=-1, keepdims=True), (t, t)) for h in range(2)]
    ones = jnp.ones((t, LANES), BF16)

    def pv_block(kb, acc):
        vaug = jnp.concatenate([v_ref[pl.ds(pl.multiple_of(kb * t, t), t), :], ones], axis=1)
        return tuple(acc[h] + _dot(jnp.exp(s_scr[h, kb] - m[h]).astype(BF16), vaug) for h in range(2))

    acc0 = jnp.zeros((t, 2 * LANES), F32)
    a0, a1 = lax.fori_loop(0, qi + 1, pv_block, (acc0, acc0))
    o = a0[:, :LANES] / a0[:, LANES:LANES + 1] - lam * (a1[:, :LANES] / a1[:, LANES:LANES + 1])
    ms = jnp.mean(o * o, axis=-1, keepdims=True)
    o_ref[...] = (o * lax.rsqrt(ms + EPS) * g_ref[...] * (1.0 - lam_init)).astype(BF16)


def _diff_attention(proj, lam_params, g, bsz, seq, col0, width, lam_init):
    t = min(DIFF_T, seq)
    nq = seq // t
    nh = width // LANES
    c0 = col0 // LANES
    small = pl.BlockSpec((1, DIFF_HEAD_DIM), lambda b, h, i: (0, 0))
    kernel = functools.partial(_diff_body, t=t, lam_init=lam_init)
    return pl.pallas_call(
        kernel,
        grid=(bsz, nh, nq),
        in_specs=[small, small, small, small,
                  pl.BlockSpec((1, LANES), lambda b, h, i: (0, 0)),
                  pl.BlockSpec((t, LANES), lambda b, h, i: (b * nq + i, c0 + h)),
                  pl.BlockSpec((seq, LANES), lambda b, h, i: (b, c0 + nh + h)),
                  pl.BlockSpec((seq, LANES), lambda b, h, i: (b, c0 + 2 * nh + h))],
        out_specs=pl.BlockSpec((t, LANES), lambda b, h, i: (b * nq + i, h)),
        out_shape=jax.ShapeDtypeStruct((bsz * seq, width), BF16),
        scratch_shapes=[pltpu.VMEM((2, nq, t, t), F32)],
        compiler_params=_params("arbitrary", "arbitrary", "arbitrary"),
        name="diff_attention",
    )(*lam_params, g, proj, proj, proj)


def _s5_body(u_ref, m_ref, ere_ref, eim_ref, fre_ref, fim_ref, are_ref, aim_ref, y_ref,
             xre_s, xim_s, sre_s, sim_s, *, bsz, nchunk, chunk):
    u = jnp.concatenate([u_ref[t] for t in range(chunk)], axis=1)
    _store_wide(xre_s, _dot(u, ere_ref[...]))
    _store_wide(xim_s, _dot(u, eim_ref[...]))
    are = are_ref[...]
    aim = aim_ref[...]
    sre = jnp.zeros((bsz, are.shape[1]), F32)
    sim = jnp.zeros((bsz, are.shape[1]), F32)
    for j in range(nchunk):
        rows = pl.ds(j, bsz, stride=nchunk)
        _store_wide(sre_s, sre, rows)
        _store_wide(sim_s, sim, rows)
        sre, sim = (are * sre - aim * sim + _load_wide(xre_s, rows),
                    are * sim + aim * sre + _load_wide(xim_s, rows))
    y = (_dot(u, m_ref[...])
         + _dot(_load_wide(sre_s).astype(BF16), fre_ref[...])
         + _dot(_load_wide(sim_s).astype(BF16), fim_ref[...]))
    for t in range(chunk):
        y_ref[t] = y[:, t * LANES:(t + 1) * LANES]


def _s5_operators(lam_re, lam_im, b_re, b_im, c_re, c_im, d_skip, log_dt, chunk):
    hp = lax.Precision.HIGHEST
    g, n = lam_re.shape
    ch = b_re.shape[-1]
    dt = jnp.exp(log_dt.astype(F32))[:, None]
    k = jnp.arange(chunk + 1, dtype=F32)[None, :, None]
    mag = jnp.exp(k * (lam_re * dt)[:, None, :])
    ang = k * (lam_im * dt)[:, None, :]
    p_re, p_im = mag * jnp.cos(ang), mag * jnp.sin(ang)
    lb_re, lb_im = p_re[:, 1], p_im[:, 1]
    den = lam_re * lam_re + lam_im * lam_im
    q_re = ((lb_re - 1.0) * lam_re + lb_im * lam_im) / den
    q_im = (lb_im * lam_re - (lb_re - 1.0) * lam_im) / den
    bb_re = q_re[:, :, None] * b_re - q_im[:, :, None] * b_im
    bb_im = q_re[:, :, None] * b_im + q_im[:, :, None] * b_re
    pb_re = p_re[:, :, :, None] * bb_re[:, None] - p_im[:, :, :, None] * bb_im[:, None]
    pb_im = p_re[:, :, :, None] * bb_im[:, None] + p_im[:, :, :, None] * bb_re[:, None]
    w = (jnp.einsum('gon,gkni->gkio', c_re, pb_re[:, :chunk], precision=hp)
         - jnp.einsum('gon,gkni->gkio', c_im, pb_im[:, :chunk], precision=hp))
    w = w.at[:, 0].add(jnp.eye(ch, dtype=F32)[None] * d_skip[:, :, None])
    s_idx = jnp.arange(chunk)[:, None]
    t_idx = jnp.arange(chunk)[None, :]
    lag = t_idx - s_idx
    m5 = jnp.where((lag >= 0)[None, :, :, None, None], w[:, jnp.clip(lag, 0, chunk - 1)], 0.0)
    gpt = LANES // ch
    nt = g // gpt
    eye = jnp.eye(gpt, dtype=F32)
    tiles = lambda x: x.reshape((nt, gpt) + x.shape[1:])
    m = jnp.einsum('aqstio,qp->asqitpo', tiles(m5), eye, precision=hp).reshape(nt, chunk * LANES, chunk * LANES)
    to_state = lambda pb: jnp.einsum('aqsni,qp->asqipn', tiles(pb[:, :chunk][:, ::-1]), eye,
                                     precision=hp).reshape(nt, chunk * LANES, gpt * n)
    e_re, e_im = to_state(pb_re), to_state(pb_im)
    cp_re = c_re[:, None] * p_re[:, 1:, None, :] - c_im[:, None] * p_im[:, 1:, None, :]
    cp_im = c_re[:, None] * p_im[:, 1:, None, :] + c_im[:, None] * p_re[:, 1:, None, :]
    from_state = lambda cp: jnp.einsum('aqton,qp->aqntpo', tiles(cp), eye,
                                       precision=hp).reshape(nt, gpt * n, chunk * LANES)
    f_re, f_im = from_state(cp_re), from_state(-cp_im)
    a_re = p_re[:, chunk].reshape(nt, 1, gpt * n)
    a_im = p_im[:, chunk].reshape(nt, 1, gpt * n)
    return (m.astype(BF16), e_re.astype(BF16), e_im.astype(BF16), f_re.astype(BF16), f_im.astype(BF16),
            a_re, a_im)


def _s5_scan(u, ops, layer, bsz, seq):
    chunk, rows, width = u.shape
    nchunk = seq // chunk
    nt = width // LANES
    kdim = chunk * LANES
    sdim = ops[1].shape[-1]
    op = lambda a, b: pl.BlockSpec((None, None, a, b), lambda i: (layer, i, 0, 0))
    act = pl.BlockSpec((chunk, rows, LANES), lambda i: (0, 0, i))
    return pl.pallas_call(
        functools.partial(_s5_body, bsz=bsz, nchunk=nchunk, chunk=chunk),
        grid=(nt,),
        in_specs=[act, op(kdim, kdim), op(kdim, sdim), op(kdim, sdim), op(sdim, kdim), op(sdim, kdim),
                  op(1, sdim), op(1, sdim)],
        out_specs=act,
        out_shape=jax.ShapeDtypeStruct((chunk, rows, width), F32),
        scratch_shapes=[pltpu.VMEM((sdim // LANES, rows, LANES), F32)] * 4,
        compiler_params=_params("arbitrary"),
        name="s5_scan",
    )(u, *ops)


def _rms(x, g):
    ms = jnp.mean(x * x, axis=-1, keepdims=True)
    return x * lax.rsqrt(ms + EPS) * g


def _outproj_body(ysb_ref, yssm_ref, ydf_ref, x_ref, sbg_ref, wglu_ref, bglu_ref, ssmg_ref, wout_ref,
                  ffng_ref, wrhi_ref, wrlo_ref, rb_ref,
                  h_ref, hn_ref, info_ref, cnt_ref, run_s, yssm_s, *, tm, sbw, ssw):
    step = pl.program_id(0)

    @pl.when(step == 0)
    def _():
        run_s[...] = jnp.zeros_like(run_s)

    ysb = _rms(ysb_ref[...].astype(F32), sbg_ref[...]).astype(BF16)
    chunk = yssm_ref.shape[0]
    for t in range(chunk):
        _store_wide(yssm_s, yssm_ref[t], pl.ds(t, tm // chunk, stride=chunk))
    y = jax.nn.gelu(_load_wide(yssm_s))
    y = y * jax.nn.sigmoid(_dot(y.astype(BF16), wglu_ref[...]) + bglu_ref[...])
    yssm = _rms(y, ssmg_ref[...]).astype(BF16)
    h = (x_ref[...]
         + _dot(ysb, wout_ref[0:sbw, :])
         + _dot(yssm, wout_ref[sbw:sbw + ssw, :])
         + _dot(ydf_ref[...], wout_ref[sbw + ssw:, :]))
    h_ref[...] = h
    hn = _rms(h, ffng_ref[...])
    _store_slabs(hn_ref, hn)

    hi, lo = _split_bf16(hn)
    logits = (_dot(hi, wrhi_ref[...]) + _dot(hi, wrlo_ref[...]) + _dot(lo, wrhi_ref[...])) + rb_ref[...]
    lane = lax.broadcasted_iota(jnp.int32, (tm, LANES), 1).astype(F32)
    ninf = -jnp.inf

    def first_max(v):
        m = jnp.max(v, axis=-1, keepdims=True)
        idx = jnp.min(jnp.where(v == m, lane, float(LANES)), axis=-1, keepdims=True)
        return m, idx

    gl = jnp.where(lane < N_GROUPS, logits, ninf)
    gmax, gidx = first_max(gl)
    g_top = 1.0 / jnp.sum(jnp.exp(gl - gmax), axis=-1, keepdims=True)
    group_lo = N_GROUPS + EXPERTS_PER_GROUP * gidx
    in_group = (lane >= group_lo) & (lane < group_lo + EXPERTS_PER_GROUP)
    el = jnp.where(in_group, logits, ninf)
    m1, i1 = first_max(el)
    m2, i2 = first_max(jnp.where(lane == i1, ninf, el))
    r = jnp.exp(m2 - m1)
    w_a = g_top / (1.0 + r)
    w_b = g_top * r / (1.0 + r)
    e_a = i1 - N_GROUPS
    e_b = i2 - N_GROUPS

    oh_a = (lane == e_a).astype(F32)
    oh_b = (lane == e_b).astype(F32)
    cnt = oh_a + oh_b
    trow = lax.broadcasted_iota(jnp.int32, (tm, tm), 0)
    tcol = lax.broadcasted_iota(jnp.int32, (tm, tm), 1)
    before = (tcol < trow).astype(BF16)
    base = _dot(before, cnt.astype(BF16)) + run_s[...]
    rank_a = jnp.sum(oh_a * base, axis=-1, keepdims=True)
    rank_b = jnp.sum(oh_b * base, axis=-1, keepdims=True)
    run_s[...] = run_s[...] + jnp.sum(cnt, axis=0, keepdims=True)
    cnt_ref[...] = run_s[...]

    info = jnp.zeros((tm, LANES), F32)
    for k, val in ((INFO_EXPERT, e_a), (INFO_EXPERT + 1, e_b), (INFO_GATE, w_a), (INFO_GATE + 1, w_b),
                   (INFO_RANK, rank_a), (INFO_RANK + 1, rank_b)):
        info = jnp.where(lane == k, val, info)
    info_ref[...] = info


def _outproj_router(ysb, yssm, ydf, x, sbg, wglu, bglu, ssmg, wout_all, ffng, wr_hi, wr_lo, rb, layer):
    t, d = x.shape
    tm = min(OUT_TM, t)
    sbw, ssw, dfw = ysb.shape[1], yssm.shape[2], ydf.shape[1]
    chunk = yssm.shape[0]
    rowblk = lambda w: pl.BlockSpec((tm, w), lambda i: (i, 0))
    const = lambda *shape: pl.BlockSpec(shape, lambda i: (0,) * len(shape))
    kernel = functools.partial(_outproj_body, tm=tm, sbw=sbw, ssw=ssw)
    return pl.pallas_call(
        kernel,
        grid=(t // tm,),
        in_specs=[rowblk(sbw), pl.BlockSpec((chunk, tm // chunk, ssw), lambda i: (0, i, 0)), rowblk(dfw), rowblk(d),
                  const(1, sbw), const(ssw, ssw), const(1, ssw), const(1, ssw),
                  pl.BlockSpec((None, d, d), lambda i: (layer, 0, 0)),
                  const(1, d), const(d, LANES), const(d, LANES), const(1, LANES)],
        out_specs=[rowblk(d), pl.BlockSpec((tm * _slab_rows(d), LANES), lambda i: (i, 0)),
                   rowblk(LANES), const(1, LANES)],
        out_shape=[jax.ShapeDtypeStruct((t, d), F32), jax.ShapeDtypeStruct((t * _slab_rows(d), LANES), F32),
                   jax.ShapeDtypeStruct((t, LANES), F32), jax.ShapeDtypeStruct((1, LANES), F32)],
        scratch_shapes=[pltpu.VMEM((1, LANES), F32), pltpu.VMEM((ssw // LANES, tm, LANES), F32)],
        compiler_params=_params("arbitrary"),
        name="outproj_router",
    )(ysb, yssm, ydf, x, sbg, wglu, bglu, ssmg, wout_all, ffng, wr_hi, wr_lo, rb)


def _moe_body(te_ref, nx_ref, nu_ref, src_ref, dst_ref,
              hn_hbm, w1_hbm, w3_hbm, w2_hbm,
              y_hbm,
              xbuf0, xbuf1, ybuf0, ybuf1, w1s, w3s, w2s, w1b, w3b, w2b, gsem, ssem, wsem, *, tm, d, layer):
    i = pl.program_id(0)
    n_used = nu_ref[0]
    xbufs = (xbuf0, xbuf1)
    ybufs = (ybuf0, ybuf1)
    s_rows = _slab_rows(d)

    def weight_copies(e):
        return [pltpu.make_async_copy(src.at[layer, e], dst, wsem.at[k])
                for k, (src, dst) in enumerate(((w1_hbm, w1s), (w3_hbm, w3s), (w2_hbm, w2s)))]

    def slab(ref, tok):
        return ref.at[pl.ds(pl.multiple_of(tok * s_rows, s_rows), s_rows), :]

    def gather_rows(tile, par, start):
        base = tile * tm
        for r in range(tm):
            cp = pltpu.make_async_copy(slab(hn_hbm, src_ref[base + r]), slab(xbufs[par], r), gsem.at[par])
            cp.start() if start else cp.wait()

    def scatter_rows(tile, par, start):
        base = (tile + 1) * tm
        for r in range(tm):
            cp = pltpu.make_async_copy(slab(ybufs[par], r), slab(y_hbm, dst_ref[base + r]), ssem.at[par])
            cp.start() if start else cp.wait()

    @pl.when(i == 0)
    def _():
        ybuf1[...] = jnp.zeros_like(ybuf1)
        for cp in weight_copies(te_ref[0]):
            cp.start(priority=1)
        gather_rows(0, 0, True)
        gather_rows(0, 0, False)

    new_expert = jnp.logical_or(i == 0, te_ref[i] != te_ref[jnp.maximum(i - 1, 0)])

    @pl.when(jnp.logical_and(i < n_used, new_expert))
    def _():
        for cp in weight_copies(te_ref[i]):
            cp.wait()
        w1b[...] = w1s[...].astype(BF16)
        w3b[...] = w3s[...].astype(BF16)
        w2b[...] = w2s[...].astype(BF16)

        @pl.when(nx_ref[i] >= 0)
        def _():
            for cp in weight_copies(nx_ref[i]):
                cp.start(priority=1)

    for par in range(2):
        active = jnp.logical_and(i < n_used, i % 2 == par)
        nxt = jnp.minimum(i + 1, n_used - 1)

        @pl.when(active)
        def _(par=par, nxt=nxt):
            x = _load_slabs(xbufs[par], tm, d).astype(BF16)
            gather_rows(nxt, 1 - par, True)
            scatter_rows(i - 1, 1 - par, True)
            h1 = _dot(x, w1b[...])
            h3 = _dot(x, w3b[...])
            a = (h1 * jax.nn.sigmoid(h1) * h3).astype(BF16)
            _store_slabs(ybufs[par], _dot(a, w2b[...]))

        @pl.when(active)
        def _(par=par, nxt=nxt):
            scatter_rows(i - 1, 1 - par, False)
            gather_rows(nxt, 1 - par, False)

        @pl.when(jnp.logical_and(active, i == n_used - 1))
        def _(par=par):
            scatter_rows(i, par, True)
            scatter_rows(i, par, False)


def _moe_experts(hn, tile_expert, next_expert, n_used, src_tok, dst_row, w1_all, w3_all, w2_all, layer):
    d = w1_all.shape[-2]
    s_rows = _slab_rows(d)
    t = hn.shape[0] // s_rows
    tm = MOE_TM
    n_tiles = tile_expert.shape[0]
    f = w1_all.shape[-1]
    hbm = pl.BlockSpec(memory_space=pl.ANY)
    grid_spec = pltpu.PrefetchScalarGridSpec(
        num_scalar_prefetch=5,
        grid=(n_tiles,),
        in_specs=[hbm, hbm, hbm, hbm],
        out_specs=hbm,
        scratch_shapes=[pltpu.VMEM((tm * s_rows, LANES), F32)] * 4
                       + [pltpu.VMEM((d, f), F32), pltpu.VMEM((d, f), F32), pltpu.VMEM((f, d), F32),
                          pltpu.VMEM((d, f), BF16), pltpu.VMEM((d, f), BF16), pltpu.VMEM((f, d), BF16),
                          pltpu.SemaphoreType.DMA((2,)), pltpu.SemaphoreType.DMA((2,)),
                          pltpu.SemaphoreType.DMA((3,))],
    )
    return pl.pallas_call(
        functools.partial(_moe_body, tm=tm, d=d, layer=layer),
        grid_spec=grid_spec,
        out_shape=jax.ShapeDtypeStruct(((2 * t + tm) * s_rows, LANES), F32),
        compiler_params=_params("arbitrary"),
        name="moe_experts",
    )(tile_expert, next_expert, n_used, src_tok, dst_row, hn, w1_all, w3_all, w2_all)


def _moe_schedule(info, counts, t):
    tm = MOE_TM
    n_tiles = (2 * t) // tm + N_EXPERTS
    p = n_tiles * tm
    e_id = info[:, INFO_EXPERT:INFO_EXPERT + 2].astype(jnp.int32)
    rank = info[:, INFO_RANK:INFO_RANK + 2].astype(jnp.int32)
    cnt = counts[0, :N_EXPERTS].astype(jnp.int32)
    tiles_e = (cnt + tm - 1) // tm
    tile_end = jnp.cumsum(tiles_e)
    tile_start = tile_end - tiles_e
    n_used = tile_end[-1]
    pos = (tile_start * tm)[e_id] + rank
    tok = jnp.broadcast_to(jnp.arange(t, dtype=jnp.int32)[:, None], (t, 2))
    dst = tok + jnp.array([0, t], jnp.int32)[None, :]
    trash = 2 * t + jnp.arange(p, dtype=jnp.int32) % tm
    default = jnp.stack([jnp.zeros((p,), jnp.int32), trash], axis=1)
    update = jnp.stack([tok.reshape(-1), dst.reshape(-1)], axis=1)
    table = default.at[pos.reshape(-1)].set(update)
    src_tok = table[:, 0]
    dst_row = jnp.concatenate([trash[:tm], table[:, 1]])
    tile_ids = jnp.minimum(jnp.arange(n_tiles, dtype=jnp.int32), n_used - 1)
    tile_expert = jnp.sum(tile_ids[:, None] >= tile_end[None, :], axis=1).astype(jnp.int32)
    after = tile_end[tile_expert]
    next_expert = jnp.where(after < n_used, tile_expert[jnp.minimum(after, n_tiles - 1)], -1).astype(jnp.int32)
    return tile_expert, next_expert, n_used.reshape(1).astype(jnp.int32), src_tok, dst_row


def _final_body(h_ref, y0_ref, y1_ref, info_ref, g_ref, o_ref):
    o_ref[...] = _rms(_moe_combine(h_ref, y0_ref, y1_ref, info_ref), g_ref[...])


def _final_norm(h, y, info, g):
    t, d = h.shape
    tm = min(FINAL_TM, t)
    nblk = t // tm
    return pl.pallas_call(
        _final_body,
        grid=(nblk,),
        in_specs=[pl.BlockSpec((tm, d), lambda i: (i, 0)),
                  pl.BlockSpec((tm * _slab_rows(d), LANES), lambda i: (i, 0)),
                  pl.BlockSpec((tm * _slab_rows(d), LANES), lambda i: (i + nblk, 0)),
                  pl.BlockSpec((tm, LANES), lambda i: (i, 0)),
                  pl.BlockSpec((1, d), lambda i: (0, 0))],
        out_specs=pl.BlockSpec((tm, d), lambda i: (i, 0)),
        out_shape=jax.ShapeDtypeStruct((t, d), F32),
        compiler_params=_params("arbitrary"),
        name="final_norm",
    )(h, y, y, info, g)


def kernel(x, norm_mix_g, w_in, sb_norm_g, ssm_lam_re, ssm_lam_im, ssm_b_re, ssm_b_im, ssm_c_re, ssm_c_im, ssm_d, ssm_log_dt, ssm_w_glu, ssm_b_glu, ssm_norm_g, diff_lq1, diff_lk1, diff_lq2, diff_lk2, diff_subln_g, w_out, norm_ffn_g, router_group_w, router_group_b, router_expert_w, router_expert_b, expert_w1, expert_w3, expert_w2, final_norm_g):
    bsz, seq, d = x.shape
    depth = w_in.shape[0]
    t = bsz * seq
    sbw = sb_norm_g.shape[-1]
    ssw = ssm_norm_g.shape[-1]
    dfw = d - sbw - ssw
    ssm_col = 3 * sbw
    diff_col = ssm_col + ssw

    w_in_b = w_in.astype(BF16)
    w_out_b = w_out.astype(BF16)
    w_glu_b = ssm_w_glu.astype(BF16)
    row = lambda v: v.reshape(1, -1).astype(F32)

    s5_ops = jax.vmap(functools.partial(_s5_operators, chunk=S5_CHUNK))(
        ssm_lam_re, ssm_lam_im, ssm_b_re, ssm_b_im, ssm_c_re, ssm_c_im, ssm_d, ssm_log_dt)

    h = x.reshape(t, d)
    moe_out = None
    for l in range(depth):
        lam_init = 0.8 - 0.6 * math.exp(-0.3 * l)
        proj, ussm, xres = _norm_inproj(h, moe_out, row(norm_mix_g[l]), w_in_b, l, ssm_col, ssw)
        ysb = _sb_attention(proj, bsz, seq, sbw)
        ydf = _diff_attention(proj, [row(p[l]) for p in (diff_lq1, diff_lk1, diff_lq2, diff_lk2)],
                              row(diff_subln_g[l]), bsz, seq, diff_col, dfw, lam_init)
        yssm = _s5_scan(ussm, s5_ops, l, bsz, seq)

        wr = jnp.zeros((d, LANES), F32)
        wr = wr.at[:, :N_GROUPS].set(router_group_w[l])
        wr = wr.at[:, N_GROUPS:N_GROUPS + N_EXPERTS].set(
            router_expert_w[l].transpose(1, 0, 2).reshape(d, N_EXPERTS))
        wr_hi, wr_lo = _split_bf16(wr)
        rb = jnp.zeros((1, LANES), F32)
        rb = rb.at[0, :N_GROUPS].set(router_group_b[l])
        rb = rb.at[0, N_GROUPS:N_GROUPS + N_EXPERTS].set(router_expert_b[l].reshape(-1))

        h, hn, info, counts = _outproj_router(
            ysb, yssm, ydf, xres, row(sb_norm_g[l]), w_glu_b[l], row(ssm_b_glu[l]), row(ssm_norm_g[l]),
            w_out_b, row(norm_ffn_g[l]), wr_hi, wr_lo, rb, l)
        sched = _moe_schedule(info, counts, t)
        moe_out = (_moe_experts(hn, *sched, expert_w1, expert_w3, expert_w2, l), info)
    out = _final_norm(h, *moe_out, row(final_norm_g))
    return out.reshape(bsz, seq, d)
```

```python
import functools
import math

import jax
import jax.numpy as jnp
from jax import lax
from jax.experimental import pallas as pl
from jax.experimental.pallas import tpu as pltpu

F32 = jnp.float32
BF16 = jnp.bfloat16

EPS = 1e-6
SB_HEAD_DIM = 64
SSM_GROUP_CH = 16
SSM_STATE = 64
DIFF_HEAD_DIM = 64
N_GROUPS = 4
EXPERTS_PER_GROUP = 8
N_EXPERTS = N_GROUPS * EXPERTS_PER_GROUP

LANES = 128
INFO_EXPERT, INFO_GATE, INFO_RANK = 0, 2, 4
VMEM_LIMIT = 56 * 1024 * 1024

INPROJ_TM = 512
INPROJ_TN = 1024
SB_TQ = 512
SB_TK = 256
DIFF_T = 512
S5_CHUNK = 8
OUT_TM = 256
MOE_TM = 256
FINAL_TM = 256


def _params(*sem):
    return pltpu.CompilerParams(dimension_semantics=sem, vmem_limit_bytes=VMEM_LIMIT)


def _dot(a, b):
    return jnp.dot(a, b, preferred_element_type=F32)


def _dot_nt(a, b):
    return lax.dot_general(a, b, (((1,), (1,)), ((), ())), preferred_element_type=F32)


def _slab_rows(d):
    return d // LANES


def _load_slabs(ref, n_tok, d):
    s_rows = _slab_rows(d)
    return jnp.concatenate([ref[pl.ds(s, n_tok, stride=s_rows), :] for s in range(s_rows)], axis=1)


def _store_slabs(ref, x):
    n_tok, d = x.shape
    s_rows = _slab_rows(d)
    for s in range(s_rows):
        ref[pl.ds(s, n_tok, stride=s_rows), :] = x[:, s * LANES:(s + 1) * LANES]


def _load_wide(ref, rows=slice(None)):
    return jnp.concatenate([ref[c, rows, :] for c in range(ref.shape[0])], axis=1)


def _store_wide(ref, x, rows=slice(None)):
    for c in range(ref.shape[0]):
        ref[c, rows, :] = x[:, c * LANES:(c + 1) * LANES]


def _split_bf16(x):
    hi = x.astype(BF16)
    lo = (x - hi.astype(F32)).astype(BF16)
    return hi, lo


def _moe_combine(h_ref, y0_ref, y1_ref, info_ref):
    info = info_ref[...]
    n_tok, d = h_ref.shape
    return (h_ref[...] + info[:, INFO_GATE:INFO_GATE + 1] * _load_slabs(y0_ref, n_tok, d)
            + info[:, INFO_GATE + 1:INFO_GATE + 2] * _load_slabs(y1_ref, n_tok, d))


def _inproj_body(*refs, after_moe, ssm_blk, ssm_off, ssm_w, chunk):
    n_in = 4 if after_moe else 1
    g_ref, w_ref = refs[n_in:n_in + 2]
    if after_moe:
        proj_ref, ussm_ref, x_out_ref, xn_ref, u_scr = refs[n_in + 2:]
    else:
        proj_ref, ussm_ref, xn_ref, u_scr = refs[n_in + 2:]

    @pl.when(pl.program_id(1) == 0)
    def _():
        if after_moe:
            x = _moe_combine(*refs[:4])
            x_out_ref[...] = x
        else:
            x = refs[0][...]
        ms = jnp.mean(x * x, axis=-1, keepdims=True)
        xn_ref[...] = (x * lax.rsqrt(ms + EPS) * g_ref[...]).astype(BF16)

    res = _dot(xn_ref[...], w_ref[...])
    proj_ref[...] = res.astype(BF16)

    @pl.when(pl.program_id(1) == ssm_blk)
    def _():
        _store_wide(u_scr, res[:, ssm_off:ssm_off + ssm_w])
        n_chunks = u_scr.shape[1] // chunk
        for t in range(chunk):
            ussm_ref[t] = _load_wide(u_scr, pl.ds(t, n_chunks, stride=chunk)).astype(BF16)


def _norm_inproj(x, moe_out, g, w_all, layer, ssm_col, ssm_w):
    t, d = x.shape
    n = w_all.shape[-1]
    tm, tn = min(INPROJ_TM, t), INPROJ_TN
    chunk = S5_CHUNK
    ssm_blk, ssm_off = divmod(ssm_col, tn)
    assert ssm_off + ssm_w <= tn and tm % chunk == 0
    rows = lambda blk, w: pl.BlockSpec((tm, w), lambda i, j: (i + blk, 0))
    args, in_specs = [x], [rows(0, d)]
    if moe_out is not None:
        y, info = moe_out
        slabs = lambda blk: pl.BlockSpec((tm * _slab_rows(d), LANES), lambda i, j: (i + blk, 0))
        args += [y, y, info]
        in_specs += [slabs(0), slabs(t // tm), rows(0, LANES)]
    in_specs += [pl.BlockSpec((1, d), lambda i, j: (0, 0)),
                 pl.BlockSpec((None, d, tn), lambda i, j: (layer, 0, j))]
    out_shape = [jax.ShapeDtypeStruct((t, n), BF16), jax.ShapeDtypeStruct((chunk, t // chunk, ssm_w), BF16)]
    out_specs = [pl.BlockSpec((tm, tn), lambda i, j: (i, j)),
                 pl.BlockSpec((chunk, tm // chunk, ssm_w), lambda i, j: (0, i, 0))]
    if moe_out is not None:
        out_shape.append(jax.ShapeDtypeStruct((t, d), F32))
        out_specs.append(pl.BlockSpec((tm, d), lambda i, j: (i, 0)))
    outs = pl.pallas_call(
        functools.partial(_inproj_body, after_moe=moe_out is not None, ssm_blk=ssm_blk, ssm_off=ssm_off,
                          ssm_w=ssm_w, chunk=chunk),
        grid=(t // tm, n // tn),
        in_specs=in_specs, out_specs=out_specs, out_shape=out_shape,
        scratch_shapes=[pltpu.VMEM((tm, d), BF16), pltpu.VMEM((ssm_w // LANES, tm, LANES), F32)],
        compiler_params=_params("arbitrary", "arbitrary"),
        name="norm_inproj",
    )(*args, g, w_all)
    return outs if moe_out is not None else (outs[0], outs[1], x)


def _sb_body(q_ref, k_ref, v_ref, o_ref, *, tq, tk):
    qi = pl.program_id(2)
    hd = SB_HEAD_DIM
    nsub = tq // tk
    lane = lax.broadcasted_iota(jnp.int32, (1, 2 * hd), 1)
    head_lanes = (lane < hd, lane >= hd)
    qs = q_ref[...] * (hd ** -0.5)
    zero = jnp.zeros((), BF16)
    qm = [jnp.where(m, qs, zero) for m in head_lanes]
    later = (lax.broadcasted_iota(jnp.int32, (tk, tk), 0)
             > lax.broadcasted_iota(jnp.int32, (tk, tk), 1)).astype(BF16)
    q_pos = qi * tq + lax.broadcasted_iota(jnp.int32, (tq, tk), 0)
    k_off = lax.broadcasted_iota(jnp.int32, (tq, tk), 1)

    def block(kb, carry, diag):
        acc, runs = carry
        start = pl.multiple_of(kb * tk, tk)
        kblk = k_ref[pl.ds(start, tk), :]
        vblk = v_ref[pl.ds(start, tk), :]
        if diag:
            strict = kb * tk + k_off < q_pos
        new_runs = []
        for h in range(2):
            z = _dot_nt(qm[h], kblk)
            log_fail = -jnp.maximum(z, 0.0) - jnp.log(1.0 + jnp.exp(-jnp.abs(z)))
            log_hit = log_fail + z
            if diag:
                log_fail = jnp.where(strict, log_fail, 0.0)
            log_after = _dot(log_fail.astype(BF16), later)
            w = jnp.exp(log_hit + log_after + runs[h])
            if diag:
                w = jnp.where(strict, w, 0.0)
            vm = jnp.where(head_lanes[h], vblk, zero)
            acc = acc + _dot(w.astype(BF16), vm)
            new_runs.append(runs[h] + jnp.sum(log_fail, axis=-1, keepdims=True))
        return acc, tuple(new_runs)

    zrun = jnp.zeros((tq, 1), F32)
    carry = (jnp.zeros((tq, 2 * hd), F32), (zrun, zrun))
    for sub in reversed(range(nsub)):
        carry = block(nsub * qi + sub, carry, True)
    carry = lax.fori_loop(0, nsub * qi, lambda it, c: block(nsub * qi - 1 - it, c, False), carry)
    o_ref[...] = carry[0].astype(BF16)


def _sb_attention(proj, bsz, seq, width):
    tq, tk = min(SB_TQ, seq), min(SB_TK, seq)
    nq = seq // tq
    npair = width // LANES
    kernel = functools.partial(_sb_body, tq=tq, tk=tk)
    return pl.pallas_call(
        kernel,
        grid=(bsz, npair, nq),
        in_specs=[pl.BlockSpec((tq, LANES), lambda b, p, i: (b * nq + i, p)),
                  pl.BlockSpec((seq, LANES), lambda b, p, i: (b, npair + p)),
                  pl.BlockSpec((seq, LANES), lambda b, p, i: (b, 2 * npair + p))],
        out_specs=pl.BlockSpec((tq, LANES), lambda b, p, i: (b * nq + i, p)),
        out_shape=jax.ShapeDtypeStruct((bsz * seq, width), BF16),
        compiler_params=_params("arbitrary", "arbitrary", "arbitrary"),
        name="sb_attention",
    )(proj, proj, proj)


def _diff_body(lq1_ref, lk1_ref, lq2_ref, lk2_ref, g_ref, q_ref, k_ref, v_ref, o_ref, s_scr, *, t, lam_init):
    qi = pl.program_id(2)
    hd = DIFF_HEAD_DIM
    lam = (jnp.exp(jnp.sum(lq1_ref[...] * lk1_ref[...], axis=-1, keepdims=True))
           - jnp.exp(jnp.sum(lq2_ref[...] * lk2_ref[...], axis=-1, keepdims=True)) + lam_init)
    lane = lax.broadcasted_iota(jnp.int32, (1, 2 * hd), 1)
    qs = q_ref[...] * (hd ** -0.5)
    zero = jnp.zeros((), BF16)
    qm = [jnp.where(lane < hd, qs, zero), jnp.where(lane >= hd, qs, zero)]
    row = lax.broadcasted_iota(jnp.int32, (t, t), 0)
    col = lax.broadcasted_iota(jnp.int32, (t, t), 1)
    causal = col <= row

    def score_block(kb, mx, diag):
        kblk = k_ref[pl.ds(pl.multiple_of(kb * t, t), t), :]
        out = []
        for h in range(2):
            s = _dot_nt(qm[h], kblk)
            if diag:
                s = jnp.where(causal, s, -jnp.inf)
            s_scr[h, kb] = s
            smax = mx[h]
            for c in range(t // LANES):
                smax = jnp.maximum(smax, s[:, c * LANES:(c + 1) * LANES])
            out.append(smax)
        return tuple(out)

    mx0 = jnp.full((t, LANES), -jnp.inf, F32)
    mx = score_block(qi, (mx0, mx0), True)
    mx = lax.fori_loop(0, qi, lambda kb, c: score_block(kb, c, False), mx)
    m = [jnp.broadcast_to(jnp.max(mx[h], axis=-1, keepdims=True), (t, t)) for h in range(2)]
    ones = jnp.ones((t, LANES), BF16)

    def pv_block(kb, acc):
        vaug = jnp.concatenate([v_ref[pl.ds(pl.multiple_of(kb * t, t), t), :], ones], axis=1)
        return tuple(acc[h] + _dot(jnp.exp(s_scr[h, kb] - m[h]).astype(BF16), vaug) for h in range(2))

    acc0 = jnp.zeros((t, 2 * LANES), F32)
    a0, a1 = lax.fori_loop(0, qi + 1, pv_block, (acc0, acc0))
    o = a0[:, :LANES] / a0[:, LANES:LANES + 1] - lam * (a1[:, :LANES] / a1[:, LANES:LANES + 1])
    ms = jnp.mean(o * o, axis=-1, keepdims=True)
    o_ref[...] = (o * lax.rsqrt(ms + EPS) * g_ref[...] * (1.0 - lam_init)).astype(BF16)


def _diff_attention(proj, lam_params, g, bsz, seq, col0, width, lam_init):
    t = min(DIFF_T, seq)
    nq = seq // t
    nh = width // LANES
    c0 = col0 // LANES
    small = pl.BlockSpec((1, DIFF_HEAD_DIM), lambda b, h, i: (0, 0))
    kernel = functools.partial(_diff_body, t=t, lam_init=lam_init)
    return pl.pallas_call(
        kernel,
        grid=(bsz, nh, nq),
        in_specs=[small, small, small, small,
                  pl.BlockSpec((1, LANES), lambda b, h, i: (0, 0)),
                  pl.BlockSpec((t, LANES), lambda b, h, i: (b * nq + i, c0 + h)),
                  pl.BlockSpec((seq, LANES), lambda b, h, i: (b, c0 + nh + h)),
                  pl.BlockSpec((seq, LANES), lambda b, h, i: (b, c0 + 2 * nh + h))],
        out_specs=pl.BlockSpec((t, LANES), lambda b, h, i: (b * nq + i, h)),
        out_shape=jax.ShapeDtypeStruct((bsz * seq, width), BF16),
        scratch_shapes=[pltpu.VMEM((2, nq, t, t), F32)],
        compiler_params=_params("arbitrary", "arbitrary", "arbitrary"),
        name="diff_attention",
    )(*lam_params, g, proj, proj, proj)


def _s5_body(u_ref, m_ref, ere_ref, eim_ref, fre_ref, fim_ref, are_ref, aim_ref, y_ref,
             xre_s, xim_s, sre_s, sim_s, *, bsz, nchunk, chunk):
    u = jnp.concatenate([u_ref[t] for t in range(chunk)], axis=1)
    _store_wide(xre_s, _dot(u, ere_ref[...]))
    _store_wide(xim_s, _dot(u, eim_ref[...]))
    are = are_ref[...]
    aim = aim_ref[...]
    sre = jnp.zeros((bsz, are.shape[1]), F32)
    sim = jnp.zeros((bsz, are.shape[1]), F32)
    for j in range(nchunk):
        rows = pl.ds(j, bsz, stride=nchunk)
        _store_wide(sre_s, sre, rows)
        _store_wide(sim_s, sim, rows)
        sre, sim = (are * sre - aim * sim + _load_wide(xre_s, rows),
                    are * sim + aim * sre + _load_wide(xim_s, rows))
    y = (_dot(u, m_ref[...])
         + _dot(_load_wide(sre_s).astype(BF16), fre_ref[...])
         + _dot(_load_wide(sim_s).astype(BF16), fim_ref[...]))
    for t in range(chunk):
        y_ref[t] = y[:, t * LANES:(t + 1) * LANES]


def _s5_operators(lam_re, lam_im, b_re, b_im, c_re, c_im, d_skip, log_dt, chunk):
    hp = lax.Precision.HIGHEST
    g, n = lam_re.shape
    ch = b_re.shape[-1]
    dt = jnp.exp(log_dt.astype(F32))[:, None]
    k = jnp.arange(chunk + 1, dtype=F32)[None, :, None]
    mag = jnp.exp(k * (lam_re * dt)[:, None, :])
    ang = k * (lam_im * dt)[:, None, :]
    p_re, p_im = mag * jnp.cos(ang), mag * jnp.sin(ang)
    lb_re, lb_im = p_re[:, 1], p_im[:, 1]
    den = lam_re * lam_re + lam_im * lam_im
    q_re = ((lb_re - 1.0) * lam_re + lb_im * lam_im) / den
    q_im = (lb_im * lam_re - (lb_re - 1.0) * lam_im) / den
    bb_re = q_re[:, :, None] * b_re - q_im[:, :, None] * b_im
    bb_im = q_re[:, :, None] * b_im + q_im[:, :, None] * b_re
    pb_re = p_re[:, :, :, None] * bb_re[:, None] - p_im[:, :, :, None] * bb_im[:, None]
    pb_im = p_re[:, :, :, None] * bb_im[:, None] + p_im[:, :, :, None] * bb_re[:, None]
    w = (jnp.einsum('gon,gkni->gkio', c_re, pb_re[:, :chunk], precision=hp)
         - jnp.einsum('gon,gkni->gkio', c_im, pb_im[:, :chunk], precision=hp))
    w = w.at[:, 0].add(jnp.eye(ch, dtype=F32)[None] * d_skip[:, :, None])
    s_idx = jnp.arange(chunk)[:, None]
    t_idx = jnp.arange(chunk)[None, :]
    lag = t_idx - s_idx
    m5 = jnp.where((lag >= 0)[None, :, :, None, None], w[:, jnp.clip(lag, 0, chunk - 1)], 0.0)
    gpt = LANES // ch
    nt = g // gpt
    eye = jnp.eye(gpt, dtype=F32)
    tiles = lambda x: x.reshape((nt, gpt) + x.shape[1:])
    m = jnp.einsum('aqstio,qp->asqitpo', tiles(m5), eye, precision=hp).reshape(nt, chunk * LANES, chunk * LANES)
    to_state = lambda pb: jnp.einsum('aqsni,qp->asqipn', tiles(pb[:, :chunk][:, ::-1]), eye,
                                     precision=hp).reshape(nt, chunk * LANES, gpt * n)
    e_re, e_im = to_state(pb_re), to_state(pb_im)
    cp_re = c_re[:, None] * p_re[:, 1:, None, :] - c_im[:, None] * p_im[:, 1:, None, :]
    cp_im = c_re[:, None] * p_im[:, 1:, None, :] + c_im[:, None] * p_re[:, 1:, None, :]
    from_state = lambda cp: jnp.einsum('aqton,qp->aqntpo', tiles(cp), eye,
                                       precision=hp).reshape(nt, gpt * n, chunk * LANES)
    f_re, f_im = from_state(cp_re), from_state(-cp_im)
    a_re = p_re[:, chunk].reshape(nt, 1, gpt * n)
    a_im = p_im[:, chunk].reshape(nt, 1, gpt * n)
    return (m.astype(BF16), e_re.astype(BF16), e_im.astype(BF16), f_re.astype(BF16), f_im.astype(BF16),
            a_re, a_im)


def _s5_scan(u, ops, bsz, seq):
    chunk, rows, width = u.shape
    nchunk = seq // chunk
    nt = width // LANES
    kdim = chunk * LANES
    sdim = ops[1].shape[-1]
    op = lambda a, b: pl.BlockSpec((None, a, b), lambda i: (i, 0, 0))
    act = pl.BlockSpec((chunk, rows, LANES), lambda i: (0, 0, i))
    return pl.pallas_call(
        functools.partial(_s5_body, bsz=bsz, nchunk=nchunk, chunk=chunk),
        grid=(nt,),
        in_specs=[act, op(kdim, kdim), op(kdim, sdim), op(kdim, sdim), op(sdim, kdim), op(sdim, kdim),
                  op(1, sdim), op(1, sdim)],
        out_specs=act,
        out_shape=jax.ShapeDtypeStruct((chunk, rows, width), F32),
        scratch_shapes=[pltpu.VMEM((sdim // LANES, rows, LANES), F32)] * 4,
        compiler_params=_params("arbitrary"),
        name="s5_scan",
    )(u, *ops)


def _rms(x, g):
    ms = jnp.mean(x * x, axis=-1, keepdims=True)
    return x * lax.rsqrt(ms + EPS) * g


def _outproj_body(ysb_ref, yssm_ref, ydf_ref, x_ref, sbg_ref, wglu_ref, bglu_ref, ssmg_ref, wout_ref,
                  ffng_ref, wrhi_ref, wrlo_ref, rb_ref,
                  h_ref, hn_ref, info_ref, cnt_ref, run_s, yssm_s, *, tm, sbw, ssw):
    step = pl.program_id(0)

    @pl.when(step == 0)
    def _():
        run_s[...] = jnp.zeros_like(run_s)

    ysb = _rms(ysb_ref[...].astype(F32), sbg_ref[...]).astype(BF16)
    chunk = yssm_ref.shape[0]
    for t in range(chunk):
        _store_wide(yssm_s, yssm_ref[t], pl.ds(t, tm // chunk, stride=chunk))
    y = jax.nn.gelu(_load_wide(yssm_s))
    y = y * jax.nn.sigmoid(_dot(y.astype(BF16), wglu_ref[...]) + bglu_ref[...])
    yssm = _rms(y, ssmg_ref[...]).astype(BF16)
    h = (x_ref[...]
         + _dot(ysb, wout_ref[0:sbw, :])
         + _dot(yssm, wout_ref[sbw:sbw + ssw, :])
         + _dot(ydf_ref[...], wout_ref[sbw + ssw:, :]))
    h_ref[...] = h
    hn = _rms(h, ffng_ref[...])
    _store_slabs(hn_ref, hn)

    hi, lo = _split_bf16(hn)
    logits = (_dot(hi, wrhi_ref[...]) + _dot(hi, wrlo_ref[...]) + _dot(lo, wrhi_ref[...])) + rb_ref[...]
    lane = lax.broadcasted_iota(jnp.int32, (tm, LANES), 1).astype(F32)
    ninf = -jnp.inf

    def first_max(v):
        m = jnp.max(v, axis=-1, keepdims=True)
        idx = jnp.min(jnp.where(v == m, lane, float(LANES)), axis=-1, keepdims=True)
        return m, idx

    gl = jnp.where(lane < N_GROUPS, logits, ninf)
    gmax, gidx = first_max(gl)
    g_top = 1.0 / jnp.sum(jnp.exp(gl - gmax), axis=-1, keepdims=True)
    group_lo = N_GROUPS + EXPERTS_PER_GROUP * gidx
    in_group = (lane >= group_lo) & (lane < group_lo + EXPERTS_PER_GROUP)
    el = jnp.where(in_group, logits, ninf)
    m1, i1 = first_max(el)
    m2, i2 = first_max(jnp.where(lane == i1, ninf, el))
    r = jnp.exp(m2 - m1)
    w_a = g_top / (1.0 + r)
    w_b = g_top * r / (1.0 + r)
    e_a = i1 - N_GROUPS
    e_b = i2 - N_GROUPS

    oh_a = (lane == e_a).astype(F32)
    oh_b = (lane == e_b).astype(F32)
    cnt = oh_a + oh_b
    trow = lax.broadcasted_iota(jnp.int32, (tm, tm), 0)
    tcol = lax.broadcasted_iota(jnp.int32, (tm, tm), 1)
    before = (tcol < trow).astype(BF16)
    base = _dot(before, cnt.astype(BF16)) + run_s[...]
    rank_a = jnp.sum(oh_a * base, axis=-1, keepdims=True)
    rank_b = jnp.sum(oh_b * base, axis=-1, keepdims=True)
    run_s[...] = run_s[...] + jnp.sum(cnt, axis=0, keepdims=True)
    cnt_ref[...] = run_s[...]

    info = jnp.zeros((tm, LANES), F32)
    for k, val in ((INFO_EXPERT, e_a), (INFO_EXPERT + 1, e_b), (INFO_GATE, w_a), (INFO_GATE + 1, w_b),
                   (INFO_RANK, rank_a), (INFO_RANK + 1, rank_b)):
        info = jnp.where(lane == k, val, info)
    info_ref[...] = info


def _outproj_router(ysb, yssm, ydf, x, sbg, wglu, bglu, ssmg, wout_all, ffng, wr_hi, wr_lo, rb, layer):
    t, d = x.shape
    tm = min(OUT_TM, t)
    sbw, ssw, dfw = ysb.shape[1], yssm.shape[2], ydf.shape[1]
    chunk = yssm.shape[0]
    rowblk = lambda w: pl.BlockSpec((tm, w), lambda i: (i, 0))
    const = lambda *shape: pl.BlockSpec(shape, lambda i: (0,) * len(shape))
    kernel = functools.partial(_outproj_body, tm=tm, sbw=sbw, ssw=ssw)
    return pl.pallas_call(
        kernel,
        grid=(t // tm,),
        in_specs=[rowblk(sbw), pl.BlockSpec((chunk, tm // chunk, ssw), lambda i: (0, i, 0)), rowblk(dfw), rowblk(d),
                  const(1, sbw), const(ssw, ssw), const(1, ssw), const(1, ssw),
                  pl.BlockSpec((None, d, d), lambda i: (layer, 0, 0)),
                  const(1, d), const(d, LANES), const(d, LANES), const(1, LANES)],
        out_specs=[rowblk(d), pl.BlockSpec((tm * _slab_rows(d), LANES), lambda i: (i, 0)),
                   rowblk(LANES), const(1, LANES)],
        out_shape=[jax.ShapeDtypeStruct((t, d), F32), jax.ShapeDtypeStruct((t * _slab_rows(d), LANES), F32),
                   jax.ShapeDtypeStruct((t, LANES), F32), jax.ShapeDtypeStruct((1, LANES), F32)],
        scratch_shapes=[pltpu.VMEM((1, LANES), F32), pltpu.VMEM((ssw // LANES, tm, LANES), F32)],
        compiler_params=_params("arbitrary"),
        name="outproj_router",
    )(ysb, yssm, ydf, x, sbg, wglu, bglu, ssmg, wout_all, ffng, wr_hi, wr_lo, rb)


def _moe_body(te_ref, nx_ref, nu_ref, src_ref, dst_ref,
              hn_hbm, w1_hbm, w3_hbm, w2_hbm,
              y_hbm,
              xbuf0, xbuf1, ybuf0, ybuf1, w1s, w3s, w2s, w1b, w3b, w2b, gsem, ssem, wsem, *, tm, d, layer):
    i = pl.program_id(0)
    n_used = nu_ref[0]
    xbufs = (xbuf0, xbuf1)
    ybufs = (ybuf0, ybuf1)
    s_rows = _slab_rows(d)

    def weight_copies(e):
        half = w2s.shape[0] // 2
        parts = ((w1_hbm.at[layer, e], w1s, 0), (w3_hbm.at[layer, e], w3s, 1),
                 (w2_hbm.at[layer, e, pl.ds(0, half), :], w2s.at[pl.ds(0, half), :], 0),
                 (w2_hbm.at[layer, e, pl.ds(half, half), :], w2s.at[pl.ds(half, half), :], 1))
        return [(pltpu.make_async_copy(src, dst, wsem.at[k]), queue) for k, (src, dst, queue) in enumerate(parts)]

    def slab(ref, tok):
        return ref.at[pl.ds(pl.multiple_of(tok * s_rows, s_rows), s_rows), :]

    def gather_rows(tile, par, start):
        base = tile * tm
        for r in range(tm):
            cp = pltpu.make_async_copy(slab(hn_hbm, src_ref[base + r]), slab(xbufs[par], r), gsem.at[par])
            cp.start() if start else cp.wait()

    def scatter_rows(tile, par, start):
        base = (tile + 1) * tm
        for r in range(tm):
            cp = pltpu.make_async_copy(slab(ybufs[par], r), slab(y_hbm, dst_ref[base + r]), ssem.at[par])
            cp.start(priority=1) if start else cp.wait()

    @pl.when(i == 0)
    def _():
        ybuf1[...] = jnp.zeros_like(ybuf1)
        for cp, queue in weight_copies(te_ref[0]):
            cp.start(priority=queue)
        gather_rows(0, 0, True)
        gather_rows(0, 0, False)

    new_expert = jnp.logical_or(i == 0, te_ref[i] != te_ref[jnp.maximum(i - 1, 0)])

    @pl.when(jnp.logical_and(i < n_used, new_expert))
    def _():
        for cp, _ in weight_copies(te_ref[i]):
            cp.wait()
        w1b[...] = w1s[...].astype(BF16)
        w3b[...] = w3s[...].astype(BF16)
        w2b[...] = w2s[...].astype(BF16)

        @pl.when(nx_ref[i] >= 0)
        def _():
            for cp, queue in weight_copies(nx_ref[i]):
                cp.start(priority=queue)

    for par in range(2):
        active = jnp.logical_and(i < n_used, i % 2 == par)
        nxt = jnp.minimum(i + 1, n_used - 1)

        @pl.when(active)
        def _(par=par, nxt=nxt):
            x = _load_slabs(xbufs[par], tm, d).astype(BF16)
            gather_rows(nxt, 1 - par, True)
            scatter_rows(i - 1, 1 - par, True)
            h1 = _dot(x, w1b[...])
            h3 = _dot(x, w3b[...])
            a = (h1 * jax.nn.sigmoid(h1) * h3).astype(BF16)
            _store_slabs(ybufs[par], _dot(a, w2b[...]))

        @pl.when(active)
        def _(par=par, nxt=nxt):
            scatter_rows(i - 1, 1 - par, False)
            gather_rows(nxt, 1 - par, False)

        @pl.when(jnp.logical_and(active, i == n_used - 1))
        def _(par=par):
            scatter_rows(i, par, True)
            scatter_rows(i, par, False)


def _moe_experts(hn, tile_expert, next_expert, n_used, src_tok, dst_row, w1_all, w3_all, w2_all, layer):
    d = w1_all.shape[-2]
    s_rows = _slab_rows(d)
    t = hn.shape[0] // s_rows
    tm = MOE_TM
    n_tiles = tile_expert.shape[0]
    f = w1_all.shape[-1]
    hbm = pl.BlockSpec(memory_space=pl.ANY)
    grid_spec = pltpu.PrefetchScalarGridSpec(
        num_scalar_prefetch=5,
        grid=(n_tiles,),
        in_specs=[hbm, hbm, hbm, hbm],
        out_specs=hbm,
        scratch_shapes=[pltpu.VMEM((tm * s_rows, LANES), F32)] * 4
                       + [pltpu.VMEM((d, f), F32), pltpu.VMEM((d, f), F32), pltpu.VMEM((f, d), F32),
                          pltpu.VMEM((d, f), BF16), pltpu.VMEM((d, f), BF16), pltpu.VMEM((f, d), BF16),
                          pltpu.SemaphoreType.DMA((2,)), pltpu.SemaphoreType.DMA((2,)),
                          pltpu.SemaphoreType.DMA((4,))],
    )
    return pl.pallas_call(
        functools.partial(_moe_body, tm=tm, d=d, layer=layer),
        grid_spec=grid_spec,
        out_shape=jax.ShapeDtypeStruct(((2 * t + tm) * s_rows, LANES), F32),
        compiler_params=_params("arbitrary"),
        name="moe_experts",
    )(tile_expert, next_expert, n_used, src_tok, dst_row, hn, w1_all, w3_all, w2_all)


def _moe_schedule(info, counts, t):
    tm = MOE_TM
    n_tiles = (2 * t) // tm + N_EXPERTS
    p = n_tiles * tm
    e_id = info[:, INFO_EXPERT:INFO_EXPERT + 2].astype(jnp.int32)
    rank = info[:, INFO_RANK:INFO_RANK + 2].astype(jnp.int32)
    cnt = counts[0, :N_EXPERTS].astype(jnp.int32)
    tiles_e = (cnt + tm - 1) // tm
    tile_end = jnp.cumsum(tiles_e)
    tile_start = tile_end - tiles_e
    n_used = tile_end[-1]
    pos = (tile_start * tm)[e_id] + rank
    tok = jnp.broadcast_to(jnp.arange(t, dtype=jnp.int32)[:, None], (t, 2))
    dst = tok + jnp.array([0, t], jnp.int32)[None, :]
    trash = 2 * t + jnp.arange(p, dtype=jnp.int32) % tm
    default = jnp.stack([jnp.zeros((p,), jnp.int32), trash], axis=1)
    update = jnp.stack([tok.reshape(-1), dst.reshape(-1)], axis=1)
    table = default.at[pos.reshape(-1)].set(update)
    src_tok = table[:, 0]
    dst_row = jnp.concatenate([trash[:tm], table[:, 1]])
    tile_ids = jnp.minimum(jnp.arange(n_tiles, dtype=jnp.int32), n_used - 1)
    tile_expert = jnp.sum(tile_ids[:, None] >= tile_end[None, :], axis=1).astype(jnp.int32)
    after = tile_end[tile_expert]
    next_expert = jnp.where(after < n_used, tile_expert[jnp.minimum(after, n_tiles - 1)], -1).astype(jnp.int32)
    return tile_expert, next_expert, n_used.reshape(1).astype(jnp.int32), src_tok, dst_row


def _final_body(h_ref, y0_ref, y1_ref, info_ref, g_ref, o_ref):
    o_ref[...] = _rms(_moe_combine(h_ref, y0_ref, y1_ref, info_ref), g_ref[...])


def _final_norm(h, y, info, g):
    t, d = h.shape
    tm = min(FINAL_TM, t)
    nblk = t // tm
    return pl.pallas_call(
        _final_body,
        grid=(nblk,),
        in_specs=[pl.BlockSpec((tm, d), lambda i: (i, 0)),
                  pl.BlockSpec((tm * _slab_rows(d), LANES), lambda i: (i, 0)),
                  pl.BlockSpec((tm * _slab_rows(d), LANES), lambda i: (i + nblk, 0)),
                  pl.BlockSpec((tm, LANES), lambda i: (i, 0)),
                  pl.BlockSpec((1, d), lambda i: (0, 0))],
        out_specs=pl.BlockSpec((tm, d), lambda i: (i, 0)),
        out_shape=jax.ShapeDtypeStruct((t, d), F32),
        compiler_params=_params("arbitrary"),
        name="final_norm",
    )(h, y, y, info, g)


def kernel(x, norm_mix_g, w_in, sb_norm_g, ssm_lam_re, ssm_lam_im, ssm_b_re, ssm_b_im, ssm_c_re, ssm_c_im, ssm_d, ssm_log_dt, ssm_w_glu, ssm_b_glu, ssm_norm_g, diff_lq1, diff_lk1, diff_lq2, diff_lk2, diff_subln_g, w_out, norm_ffn_g, router_group_w, router_group_b, router_expert_w, router_expert_b, expert_w1, expert_w3, expert_w2, final_norm_g):
    bsz, seq, d = x.shape
    depth = w_in.shape[0]
    t = bsz * seq
    sbw = sb_norm_g.shape[-1]
    ssw = ssm_norm_g.shape[-1]
    dfw = d - sbw - ssw
    ssm_col = 3 * sbw
    diff_col = ssm_col + ssw

    w_in_b = w_in.astype(BF16)
    w_out_b = w_out.astype(BF16)
    w_glu_b = ssm_w_glu.astype(BF16)
    row = lambda v: v.reshape(1, -1).astype(F32)

    h = x.reshape(t, d)
    moe_out = None
    for l in range(depth):
        lam_init = 0.8 - 0.6 * math.exp(-0.3 * l)
        proj, ussm, xres = _norm_inproj(h, moe_out, row(norm_mix_g[l]), w_in_b, l, ssm_col, ssw)
        ysb = _sb_attention(proj, bsz, seq, sbw)
        ydf = _diff_attention(proj, [row(p[l]) for p in (diff_lq1, diff_lk1, diff_lq2, diff_lk2)],
                              row(diff_subln_g[l]), bsz, seq, diff_col, dfw, lam_init)
        s5_ops = _s5_operators(ssm_lam_re[l], ssm_lam_im[l], ssm_b_re[l], ssm_b_im[l], ssm_c_re[l], ssm_c_im[l],
                               ssm_d[l], ssm_log_dt[l], S5_CHUNK)
        yssm = _s5_scan(ussm, s5_ops, bsz, seq)

        wr = jnp.zeros((d, LANES), F32)
        wr = wr.at[:, :N_GROUPS].set(router_group_w[l])
        wr = wr.at[:, N_GROUPS:N_GROUPS + N_EXPERTS].set(
            router_expert_w[l].transpose(1, 0, 2).reshape(d, N_EXPERTS))
        wr_hi, wr_lo = _split_bf16(wr)
        rb = jnp.zeros((1, LANES), F32)
        rb = rb.at[0, :N_GROUPS].set(router_group_b[l])
        rb = rb.at[0, N_GROUPS:N_GROUPS + N_EXPERTS].set(router_expert_b[l].reshape(-1))

        h, hn, info, counts = _outproj_router(
            ysb, yssm, ydf, xres, row(sb_norm_g[l]), w_glu_b[l], row(ssm_b_glu[l]), row(ssm_norm_g[l]),
            w_out_b, row(norm_ffn_g[l]), wr_hi, wr_lo, rb, l)
        sched = _moe_schedule(info, counts, t)
        moe_out = (_moe_experts(hn, *sched, expert_w1, expert_w3, expert_w2, l), info)
    out = _final_norm(h, *moe_out, row(final_norm_g))
    return out.reshape(bsz, seq, d)
```

```python
import functools
import math

import numpy as np
import jax
import jax.numpy as jnp
from jax import lax
from jax.experimental import pallas as pl
from jax.experimental.pallas import tpu as pltpu

F32 = jnp.float32
BF16 = jnp.bfloat16

EPS = 1e-6
SB_HEAD_DIM = 64
SSM_GROUP_CH = 16
SSM_STATE = 64
DIFF_HEAD_DIM = 64
N_GROUPS = 4
EXPERTS_PER_GROUP = 8
N_EXPERTS = N_GROUPS * EXPERTS_PER_GROUP

LANES = 128
INFO_EXPERT, INFO_GATE, INFO_RANK = 0, 2, 4
VMEM_LIMIT = 56 * 1024 * 1024

INPROJ_TM = 512
INPROJ_TN = 1024
SB_TQ = 512
SB_TK = 256
DIFF_T = 512
S5_CHUNK = 8
OUT_TM = 256
MOE_TM = 256
FINAL_TM = 256


def _params(*sem):
    return pltpu.CompilerParams(dimension_semantics=sem, vmem_limit_bytes=VMEM_LIMIT)


def _dot(a, b):
    return jnp.dot(a, b, preferred_element_type=F32)


def _dot_nt(a, b):
    return lax.dot_general(a, b, (((1,), (1,)), ((), ())), preferred_element_type=F32)


def _slab_rows(d):
    return d // LANES


def _load_slabs(ref, n_tok, d):
    s_rows = _slab_rows(d)
    return jnp.concatenate([ref[pl.ds(s, n_tok, stride=s_rows), :] for s in range(s_rows)], axis=1)


def _store_slabs(ref, x):
    n_tok, d = x.shape
    s_rows = _slab_rows(d)
    for s in range(s_rows):
        ref[pl.ds(s, n_tok, stride=s_rows), :] = x[:, s * LANES:(s + 1) * LANES]


def _load_wide(ref, rows=slice(None)):
    return jnp.concatenate([ref[c, rows, :] for c in range(ref.shape[0])], axis=1)


def _store_wide(ref, x, rows=slice(None)):
    for c in range(ref.shape[0]):
        ref[c, rows, :] = x[:, c * LANES:(c + 1) * LANES]


def _split_bf16(x):
    hi = x.astype(BF16)
    lo = (x - hi.astype(F32)).astype(BF16)
    return hi, lo


def _moe_combine(h_ref, y0_ref, y1_ref, info_ref):
    info = info_ref[...]
    n_tok, d = h_ref.shape
    return (h_ref[...] + info[:, INFO_GATE:INFO_GATE + 1] * _load_slabs(y0_ref, n_tok, d)
            + info[:, INFO_GATE + 1:INFO_GATE + 2] * _load_slabs(y1_ref, n_tok, d))


def _inproj_body(*refs, after_moe, ssm_blk, ssm_off, ssm_w, chunk):
    n_in = 4 if after_moe else 1
    g_ref, w_ref = refs[n_in:n_in + 2]
    if after_moe:
        proj_ref, ussm_ref, x_out_ref, xn_ref, u_scr = refs[n_in + 2:]
    else:
        proj_ref, ussm_ref, xn_ref, u_scr = refs[n_in + 2:]

    @pl.when(pl.program_id(1) == 0)
    def _():
        if after_moe:
            x = _moe_combine(*refs[:4])
            x_out_ref[...] = x
        else:
            x = refs[0][...]
        ms = jnp.mean(x * x, axis=-1, keepdims=True)
        xn_ref[...] = (x * lax.rsqrt(ms + EPS) * g_ref[...]).astype(BF16)

    res = _dot(xn_ref[...], w_ref[...])
    proj_ref[...] = res.astype(BF16)

    @pl.when(pl.program_id(1) == ssm_blk)
    def _():
        _store_wide(u_scr, res[:, ssm_off:ssm_off + ssm_w])
        n_chunks = u_scr.shape[1] // chunk
        for t in range(chunk):
            ussm_ref[t] = _load_wide(u_scr, pl.ds(t, n_chunks, stride=chunk)).astype(BF16)


def _norm_inproj(x, moe_out, g, w_all, layer, ssm_col, ssm_w):
    t, d = x.shape
    n = w_all.shape[-1]
    tm, tn = min(INPROJ_TM, t), INPROJ_TN
    chunk = S5_CHUNK
    ssm_blk, ssm_off = divmod(ssm_col, tn)
    assert ssm_off + ssm_w <= tn and tm % chunk == 0
    rows = lambda blk, w: pl.BlockSpec((tm, w), lambda i, j: (i + blk, 0))
    args, in_specs = [x], [rows(0, d)]
    if moe_out is not None:
        y, info = moe_out
        slabs = lambda blk: pl.BlockSpec((tm * _slab_rows(d), LANES), lambda i, j: (i + blk, 0))
        args += [y, y, info]
        in_specs += [slabs(0), slabs(t // tm), rows(0, LANES)]
    in_specs += [pl.BlockSpec((1, d), lambda i, j: (0, 0)),
                 pl.BlockSpec((None, d, tn), lambda i, j: (layer, 0, j))]
    out_shape = [jax.ShapeDtypeStruct((t, n), BF16), jax.ShapeDtypeStruct((chunk, t // chunk, ssm_w), BF16)]
    out_specs = [pl.BlockSpec((tm, tn), lambda i, j: (i, j)),
                 pl.BlockSpec((chunk, tm // chunk, ssm_w), lambda i, j: (0, i, 0))]
    if moe_out is not None:
        out_shape.append(jax.ShapeDtypeStruct((t, d), F32))
        out_specs.append(pl.BlockSpec((tm, d), lambda i, j: (i, 0)))
    outs = pl.pallas_call(
        functools.partial(_inproj_body, after_moe=moe_out is not None, ssm_blk=ssm_blk, ssm_off=ssm_off,
                          ssm_w=ssm_w, chunk=chunk),
        grid=(t // tm, n // tn),
        in_specs=in_specs, out_specs=out_specs, out_shape=out_shape,
        scratch_shapes=[pltpu.VMEM((tm, d), BF16), pltpu.VMEM((ssm_w // LANES, tm, LANES), F32)],
        compiler_params=_params("arbitrary", "arbitrary"),
        name="norm_inproj",
    )(*args, g, w_all)
    return outs if moe_out is not None else (outs[0], outs[1], x)


def _sb_body(q_ref, k_ref, v_ref, o_ref, *, tq, tk):
    qi = pl.program_id(2)
    hd = SB_HEAD_DIM
    nsub = tq // tk
    lane = lax.broadcasted_iota(jnp.int32, (1, 2 * hd), 1)
    head_lanes = (lane < hd, lane >= hd)
    qs = q_ref[...] * (hd ** -0.5)
    zero = jnp.zeros((), BF16)
    qm = [jnp.where(m, qs, zero) for m in head_lanes]
    later = (lax.broadcasted_iota(jnp.int32, (tk, tk), 0)
             > lax.broadcasted_iota(jnp.int32, (tk, tk), 1)).astype(BF16)
    q_pos = qi * tq + lax.broadcasted_iota(jnp.int32, (tq, tk), 0)
    k_off = lax.broadcasted_iota(jnp.int32, (tq, tk), 1)

    def block(kb, carry, diag):
        acc, runs = carry
        start = pl.multiple_of(kb * tk, tk)
        kblk = k_ref[pl.ds(start, tk), :]
        vblk = v_ref[pl.ds(start, tk), :]
        if diag:
            strict = kb * tk + k_off < q_pos
        new_runs = []
        for h in range(2):
            z = _dot_nt(qm[h], kblk)
            log_fail = -jnp.maximum(z, 0.0) - jnp.log(1.0 + jnp.exp(-jnp.abs(z)))
            log_hit = log_fail + z
            if diag:
                log_fail = jnp.where(strict, log_fail, 0.0)
            log_after = _dot(log_fail.astype(BF16), later)
            w = jnp.exp(log_hit + log_after + runs[h])
            if diag:
                w = jnp.where(strict, w, 0.0)
            vm = jnp.where(head_lanes[h], vblk, zero)
            acc = acc + _dot(w.astype(BF16), vm)
            new_runs.append(runs[h] + jnp.sum(log_fail, axis=-1, keepdims=True))
        return acc, tuple(new_runs)

    zrun = jnp.zeros((tq, 1), F32)
    carry = (jnp.zeros((tq, 2 * hd), F32), (zrun, zrun))
    for sub in reversed(range(nsub)):
        carry = block(nsub * qi + sub, carry, True)
    carry = lax.fori_loop(0, nsub * qi, lambda it, c: block(nsub * qi - 1 - it, c, False), carry)
    o_ref[...] = carry[0].astype(BF16)


def _sb_attention(proj, bsz, seq, width):
    tq, tk = min(SB_TQ, seq), min(SB_TK, seq)
    nq = seq // tq
    npair = width // LANES
    kernel = functools.partial(_sb_body, tq=tq, tk=tk)
    return pl.pallas_call(
        kernel,
        grid=(bsz, npair, nq),
        in_specs=[pl.BlockSpec((tq, LANES), lambda b, p, i: (b * nq + i, p)),
                  pl.BlockSpec((seq, LANES), lambda b, p, i: (b, npair + p)),
                  pl.BlockSpec((seq, LANES), lambda b, p, i: (b, 2 * npair + p))],
        out_specs=pl.BlockSpec((tq, LANES), lambda b, p, i: (b * nq + i, p)),
        out_shape=jax.ShapeDtypeStruct((bsz * seq, width), BF16),
        compiler_params=_params("arbitrary", "arbitrary", "arbitrary"),
        name="sb_attention",
    )(proj, proj, proj)


def _diff_body(lq1_ref, lk1_ref, lq2_ref, lk2_ref, g_ref, q_ref, k_ref, v_ref, o_ref, s_scr, *, t, lam_init):
    qi = pl.program_id(2)
    hd = DIFF_HEAD_DIM
    lam = (jnp.exp(jnp.sum(lq1_ref[...] * lk1_ref[...], axis=-1, keepdims=True))
           - jnp.exp(jnp.sum(lq2_ref[...] * lk2_ref[...], axis=-1, keepdims=True)) + lam_init)
    lane = lax.broadcasted_iota(jnp.int32, (1, 2 * hd), 1)
    qs = q_ref[...] * (hd ** -0.5)
    zero = jnp.zeros((), BF16)
    qm = [jnp.where(lane < hd, qs, zero), jnp.where(lane >= hd, qs, zero)]
    row = lax.broadcasted_iota(jnp.int32, (t, t), 0)
    col = lax.broadcasted_iota(jnp.int32, (t, t), 1)
    causal = col <= row

    def score_block(kb, mx, diag):
        kblk = k_ref[pl.ds(pl.multiple_of(kb * t, t), t), :]
        out = []
        for h in range(2):
            s = _dot_nt(qm[h], kblk)
            if diag:
                s = jnp.where(causal, s, -jnp.inf)
            s_scr[h, kb] = s
            smax = mx[h]
            for c in range(t // LANES):
                smax = jnp.maximum(smax, s[:, c * LANES:(c + 1) * LANES])
            out.append(smax)
        return tuple(out)

    mx0 = jnp.full((t, LANES), -jnp.inf, F32)
    mx = score_block(qi, (mx0, mx0), True)
    mx = lax.fori_loop(0, qi, lambda kb, c: score_block(kb, c, False), mx)
    m = [jnp.broadcast_to(jnp.max(mx[h], axis=-1, keepdims=True), (t, t)) for h in range(2)]
    ones = jnp.ones((t, LANES), BF16)

    def pv_block(kb, acc):
        vaug = jnp.concatenate([v_ref[pl.ds(pl.multiple_of(kb * t, t), t), :], ones], axis=1)
        return tuple(acc[h] + _dot(jnp.exp(s_scr[h, kb] - m[h]).astype(BF16), vaug) for h in range(2))

    acc0 = jnp.zeros((t, 2 * LANES), F32)
    a0, a1 = lax.fori_loop(0, qi + 1, pv_block, (acc0, acc0))
    o = a0[:, :LANES] / a0[:, LANES:LANES + 1] - lam * (a1[:, :LANES] / a1[:, LANES:LANES + 1])
    ms = jnp.mean(o * o, axis=-1, keepdims=True)
    o_ref[...] = (o * lax.rsqrt(ms + EPS) * g_ref[...] * (1.0 - lam_init)).astype(BF16)


def _diff_attention(proj, lam_params, g, bsz, seq, col0, width, lam_init):
    t = min(DIFF_T, seq)
    nq = seq // t
    nh = width // LANES
    c0 = col0 // LANES
    small = pl.BlockSpec((1, DIFF_HEAD_DIM), lambda b, h, i: (0, 0))
    kernel = functools.partial(_diff_body, t=t, lam_init=lam_init)
    return pl.pallas_call(
        kernel,
        grid=(bsz, nh, nq),
        in_specs=[small, small, small, small,
                  pl.BlockSpec((1, LANES), lambda b, h, i: (0, 0)),
                  pl.BlockSpec((t, LANES), lambda b, h, i: (b * nq + i, c0 + h)),
                  pl.BlockSpec((seq, LANES), lambda b, h, i: (b, c0 + nh + h)),
                  pl.BlockSpec((seq, LANES), lambda b, h, i: (b, c0 + 2 * nh + h))],
        out_specs=pl.BlockSpec((t, LANES), lambda b, h, i: (b * nq + i, h)),
        out_shape=jax.ShapeDtypeStruct((bsz * seq, width), BF16),
        scratch_shapes=[pltpu.VMEM((2, nq, t, t), F32)],
        compiler_params=_params("arbitrary", "arbitrary", "arbitrary"),
        name="diff_attention",
    )(*lam_params, g, proj, proj, proj)


def _s5_body(u_ref, m_ref, ere_ref, eim_ref, fre_ref, fim_ref, are_ref, aim_ref, y_ref,
             xre_s, xim_s, sre_s, sim_s, *, bsz, nchunk, chunk):
    u = jnp.concatenate([u_ref[t] for t in range(chunk)], axis=1)
    _store_wide(xre_s, _dot(u, ere_ref[...]))
    _store_wide(xim_s, _dot(u, eim_ref[...]))
    are = are_ref[...]
    aim = aim_ref[...]
    sre = jnp.zeros((bsz, are.shape[1]), F32)
    sim = jnp.zeros((bsz, are.shape[1]), F32)
    for j in range(nchunk):
        rows = pl.ds(j, bsz, stride=nchunk)
        _store_wide(sre_s, sre, rows)
        _store_wide(sim_s, sim, rows)
        sre, sim = (are * sre - aim * sim + _load_wide(xre_s, rows),
                    are * sim + aim * sre + _load_wide(xim_s, rows))
    y = (_dot(u, m_ref[...])
         + _dot(_load_wide(sre_s).astype(BF16), fre_ref[...])
         + _dot(_load_wide(sim_s).astype(BF16), fim_ref[...]))
    for t in range(chunk):
        y_ref[t] = y[:, t * LANES:(t + 1) * LANES]


def _s5_operators(lam_re, lam_im, b_re, b_im, c_re, c_im, d_skip, log_dt, chunk):
    hp = lax.Precision.HIGHEST
    g, n = lam_re.shape
    ch = b_re.shape[-1]
    dt = jnp.exp(log_dt.astype(F32))[:, None]
    k = jnp.arange(chunk + 1, dtype=F32)[None, :, None]
    mag = jnp.exp(k * (lam_re * dt)[:, None, :])
    ang = k * (lam_im * dt)[:, None, :]
    p_re, p_im = mag * jnp.cos(ang), mag * jnp.sin(ang)
    lb_re, lb_im = p_re[:, 1], p_im[:, 1]
    den = lam_re * lam_re + lam_im * lam_im
    q_re = ((lb_re - 1.0) * lam_re + lb_im * lam_im) / den
    q_im = (lb_im * lam_re - (lb_re - 1.0) * lam_im) / den
    bb_re = q_re[:, :, None] * b_re - q_im[:, :, None] * b_im
    bb_im = q_re[:, :, None] * b_im + q_im[:, :, None] * b_re
    pb_re = p_re[:, :, :, None] * bb_re[:, None] - p_im[:, :, :, None] * bb_im[:, None]
    pb_im = p_re[:, :, :, None] * bb_im[:, None] + p_im[:, :, :, None] * bb_re[:, None]
    w = (jnp.einsum('gon,gkni->gkio', c_re, pb_re[:, :chunk], precision=hp)
         - jnp.einsum('gon,gkni->gkio', c_im, pb_im[:, :chunk], precision=hp))
    w = w.at[:, 0].add(jnp.eye(ch, dtype=F32)[None] * d_skip[:, :, None])
    gpt = LANES // ch
    nt = g // gpt

    def block_diag(x, rdim, cdim):
        rep = jnp.asarray(np.tile(np.eye(cdim, dtype=np.float32), (1, gpt)))
        mask = jnp.asarray((np.arange(gpt * rdim)[:, None] // rdim == np.arange(gpt * cdim)[None, :] // cdim)
                           .astype(np.float32))
        return jnp.einsum('akrc,cl->akrl', x, rep, precision=hp) * mask

    def per_tile(x, perm):
        x = x.reshape(nt, gpt, x.shape[1], x.shape[2], x.shape[3])
        x = x.transpose(0, 2, 1, 4, 3) if perm else x.transpose(0, 2, 1, 3, 4)
        return x.reshape(nt, x.shape[1], gpt * x.shape[3], x.shape[4])

    bd = block_diag(per_tile(w, False), ch, ch).astype(BF16)
    zero = jnp.zeros_like(bd[:, 0])
    m = jnp.concatenate([jnp.concatenate([bd[:, t - s] if t >= s else zero for t in range(chunk)], axis=2)
                         for s in range(chunk)], axis=1)
    to_state = lambda pb: block_diag(per_tile(pb[:, :chunk][:, ::-1], True), ch, n).astype(BF16).reshape(
        nt, chunk * LANES, gpt * n)
    e_re, e_im = to_state(pb_re), to_state(pb_im)
    cp_re = c_re[:, None] * p_re[:, 1:, None, :] - c_im[:, None] * p_im[:, 1:, None, :]
    cp_im = c_re[:, None] * p_im[:, 1:, None, :] + c_im[:, None] * p_re[:, 1:, None, :]

    def from_state(cp):
        blocks = block_diag(per_tile(cp, True), n, ch).astype(BF16)
        return jnp.concatenate([blocks[:, t] for t in range(chunk)], axis=2)

    f_re, f_im = from_state(cp_re), from_state(-cp_im)
    a_re = p_re[:, chunk].reshape(nt, 1, gpt * n)
    a_im = p_im[:, chunk].reshape(nt, 1, gpt * n)
    return m, e_re, e_im, f_re, f_im, a_re, a_im


def _s5_scan(u, ops, bsz, seq):
    chunk, rows, width = u.shape
    nchunk = seq // chunk
    nt = width // LANES
    kdim = chunk * LANES
    sdim = ops[1].shape[-1]
    op = lambda a, b: pl.BlockSpec((None, a, b), lambda i: (i, 0, 0))
    act = pl.BlockSpec((chunk, rows, LANES), lambda i: (0, 0, i))
    return pl.pallas_call(
        functools.partial(_s5_body, bsz=bsz, nchunk=nchunk, chunk=chunk),
        grid=(nt,),
        in_specs=[act, op(kdim, kdim), op(kdim, sdim), op(kdim, sdim), op(sdim, kdim), op(sdim, kdim),
                  op(1, sdim), op(1, sdim)],
        out_specs=act,
        out_shape=jax.ShapeDtypeStruct((chunk, rows, width), F32),
        scratch_shapes=[pltpu.VMEM((sdim // LANES, rows, LANES), F32)] * 4,
        compiler_params=_params("arbitrary"),
        name="s5_scan",
    )(u, *ops)


def _rms(x, g):
    ms = jnp.mean(x * x, axis=-1, keepdims=True)
    return x * lax.rsqrt(ms + EPS) * g


def _outproj_body(ysb_ref, yssm_ref, ydf_ref, x_ref, sbg_ref, wglu_ref, bglu_ref, ssmg_ref, wout_ref,
                  ffng_ref, wrhi_ref, wrlo_ref, rb_ref,
                  h_ref, hn_ref, info_ref, cnt_ref, run_s, yssm_s, *, tm, sbw, ssw):
    step = pl.program_id(0)

    @pl.when(step == 0)
    def _():
        run_s[...] = jnp.zeros_like(run_s)

    ysb = _rms(ysb_ref[...].astype(F32), sbg_ref[...]).astype(BF16)
    chunk = yssm_ref.shape[0]
    for t in range(chunk):
        _store_wide(yssm_s, yssm_ref[t], pl.ds(t, tm // chunk, stride=chunk))
    y = jax.nn.gelu(_load_wide(yssm_s))
    y = y * jax.nn.sigmoid(_dot(y.astype(BF16), wglu_ref[...]) + bglu_ref[...])
    yssm = _rms(y, ssmg_ref[...]).astype(BF16)
    h = (x_ref[...]
         + _dot(ysb, wout_ref[0:sbw, :])
         + _dot(yssm, wout_ref[sbw:sbw + ssw, :])
         + _dot(ydf_ref[...], wout_ref[sbw + ssw:, :]))
    h_ref[...] = h
    hn = _rms(h, ffng_ref[...])
    _store_slabs(hn_ref, hn)

    hi, lo = _split_bf16(hn)
    logits = (_dot(hi, wrhi_ref[...]) + _dot(hi, wrlo_ref[...]) + _dot(lo, wrhi_ref[...])) + rb_ref[...]
    lane = lax.broadcasted_iota(jnp.int32, (tm, LANES), 1).astype(F32)
    ninf = -jnp.inf

    def first_max(v):
        m = jnp.max(v, axis=-1, keepdims=True)
        idx = jnp.min(jnp.where(v == m, lane, float(LANES)), axis=-1, keepdims=True)
        return m, idx

    gl = jnp.where(lane < N_GROUPS, logits, ninf)
    gmax, gidx = first_max(gl)
    g_top = 1.0 / jnp.sum(jnp.exp(gl - gmax), axis=-1, keepdims=True)
    group_lo = N_GROUPS + EXPERTS_PER_GROUP * gidx
    in_group = (lane >= group_lo) & (lane < group_lo + EXPERTS_PER_GROUP)
    el = jnp.where(in_group, logits, ninf)
    m1, i1 = first_max(el)
    m2, i2 = first_max(jnp.where(lane == i1, ninf, el))
    r = jnp.exp(m2 - m1)
    w_a = g_top / (1.0 + r)
    w_b = g_top * r / (1.0 + r)
    e_a = i1 - N_GROUPS
    e_b = i2 - N_GROUPS

    oh_a = (lane == e_a).astype(F32)
    oh_b = (lane == e_b).astype(F32)
    cnt = oh_a + oh_b
    trow = lax.broadcasted_iota(jnp.int32, (tm, tm), 0)
    tcol = lax.broadcasted_iota(jnp.int32, (tm, tm), 1)
    before = (tcol < trow).astype(BF16)
    base = _dot(before, cnt.astype(BF16)) + run_s[...]
    rank_a = jnp.sum(oh_a * base, axis=-1, keepdims=True)
    rank_b = jnp.sum(oh_b * base, axis=-1, keepdims=True)
    run_s[...] = run_s[...] + jnp.sum(cnt, axis=0, keepdims=True)
    cnt_ref[...] = run_s[...]

    info = jnp.zeros((tm, LANES), F32)
    for k, val in ((INFO_EXPERT, e_a), (INFO_EXPERT + 1, e_b), (INFO_GATE, w_a), (INFO_GATE + 1, w_b),
                   (INFO_RANK, rank_a), (INFO_RANK + 1, rank_b)):
        info = jnp.where(lane == k, val, info)
    info_ref[...] = info


def _outproj_router(ysb, yssm, ydf, x, sbg, wglu, bglu, ssmg, wout_all, ffng, wr_hi, wr_lo, rb, layer):
    t, d = x.shape
    tm = min(OUT_TM, t)
    sbw, ssw, dfw = ysb.shape[1], yssm.shape[2], ydf.shape[1]
    chunk = yssm.shape[0]
    rowblk = lambda w: pl.BlockSpec((tm, w), lambda i: (i, 0))
    const = lambda *shape: pl.BlockSpec(shape, lambda i: (0,) * len(shape))
    kernel = functools.partial(_outproj_body, tm=tm, sbw=sbw, ssw=ssw)
    return pl.pallas_call(
        kernel,
        grid=(t // tm,),
        in_specs=[rowblk(sbw), pl.BlockSpec((chunk, tm // chunk, ssw), lambda i: (0, i, 0)), rowblk(dfw), rowblk(d),
                  const(1, sbw), const(ssw, ssw), const(1, ssw), const(1, ssw),
                  pl.BlockSpec((None, d, d), lambda i: (layer, 0, 0)),
                  const(1, d), const(d, LANES), const(d, LANES), const(1, LANES)],
        out_specs=[rowblk(d), pl.BlockSpec((tm * _slab_rows(d), LANES), lambda i: (i, 0)),
                   rowblk(LANES), const(1, LANES)],
        out_shape=[jax.ShapeDtypeStruct((t, d), F32), jax.ShapeDtypeStruct((t * _slab_rows(d), LANES), F32),
                   jax.ShapeDtypeStruct((t, LANES), F32), jax.ShapeDtypeStruct((1, LANES), F32)],
        scratch_shapes=[pltpu.VMEM((1, LANES), F32), pltpu.VMEM((ssw // LANES, tm, LANES), F32)],
        compiler_params=_params("arbitrary"),
        name="outproj_router",
    )(ysb, yssm, ydf, x, sbg, wglu, bglu, ssmg, wout_all, ffng, wr_hi, wr_lo, rb)


def _moe_body(te_ref, nx_ref, nu_ref, src_ref, dst_ref,
              hn_hbm, w1_hbm, w3_hbm, w2_hbm,
              y_hbm,
              xbuf0, xbuf1, ybuf0, ybuf1, w1s, w3s, w2s, w1b, w3b, w2b, gsem, ssem, wsem, *, tm, d, layer):
    i = pl.program_id(0)
    n_used = nu_ref[0]
    xbufs = (xbuf0, xbuf1)
    ybufs = (ybuf0, ybuf1)
    s_rows = _slab_rows(d)

    def weight_copies(e):
        half = w2s.shape[0] // 2
        parts = ((w1_hbm.at[layer, e], w1s, 0), (w3_hbm.at[layer, e], w3s, 1),
                 (w2_hbm.at[layer, e, pl.ds(0, half), :], w2s.at[pl.ds(0, half), :], 0),
                 (w2_hbm.at[layer, e, pl.ds(half, half), :], w2s.at[pl.ds(half, half), :], 1))
        return [(pltpu.make_async_copy(src, dst, wsem.at[k]), queue) for k, (src, dst, queue) in enumerate(parts)]

    def slab(ref, tok):
        return ref.at[pl.ds(pl.multiple_of(tok * s_rows, s_rows), s_rows), :]

    def gather_rows(tile, par, start):
        if not start:
            pltpu.make_async_copy(hn_hbm.at[pl.ds(0, tm * s_rows), :], xbufs[par], gsem.at[par]).wait()
            return
        base = tile * tm
        for r in range(tm):
            pltpu.make_async_copy(slab(hn_hbm, src_ref[base + r]), slab(xbufs[par], r), gsem.at[par]).start()

    def scatter_rows(tile, par, start):
        if not start:
            pltpu.make_async_copy(ybufs[par], y_hbm.at[pl.ds(0, tm * s_rows), :], ssem.at[par]).wait()
            return
        base = (tile + 1) * tm
        for r in range(tm):
            pltpu.make_async_copy(slab(ybufs[par], r), slab(y_hbm, dst_ref[base + r]), ssem.at[par]).start(priority=1)

    @pl.when(i == 0)
    def _():
        ybuf1[...] = jnp.zeros_like(ybuf1)
        for cp, queue in weight_copies(te_ref[0]):
            cp.start(priority=queue)
        gather_rows(0, 0, True)
        gather_rows(0, 0, False)

    new_expert = jnp.logical_or(i == 0, te_ref[i] != te_ref[jnp.maximum(i - 1, 0)])

    @pl.when(jnp.logical_and(i < n_used, new_expert))
    def _():
        for cp, _ in weight_copies(te_ref[i]):
            cp.wait()
        w1b[...] = w1s[...].astype(BF16)
        w3b[...] = w3s[...].astype(BF16)
        w2b[...] = w2s[...].astype(BF16)

        @pl.when(nx_ref[i] >= 0)
        def _():
            for cp, queue in weight_copies(nx_ref[i]):
                cp.start(priority=queue)

    for par in range(2):
        active = jnp.logical_and(i < n_used, i % 2 == par)
        nxt = jnp.minimum(i + 1, n_used - 1)

        @pl.when(active)
        def _(par=par, nxt=nxt):
            x = _load_slabs(xbufs[par], tm, d).astype(BF16)
            gather_rows(nxt, 1 - par, True)
            scatter_rows(i - 1, 1 - par, True)
            h1 = _dot(x, w1b[...])
            h3 = _dot(x, w3b[...])
            a = (h1 * jax.nn.sigmoid(h1) * h3).astype(BF16)
            _store_slabs(ybufs[par], _dot(a, w2b[...]))

        @pl.when(active)
        def _(par=par, nxt=nxt):
            scatter_rows(i - 1, 1 - par, False)
            gather_rows(nxt, 1 - par, False)

        @pl.when(jnp.logical_and(active, i == n_used - 1))
        def _(par=par):
            scatter_rows(i, par, True)
            scatter_rows(i, par, False)


def _moe_experts(hn, tile_expert, next_expert, n_used, src_tok, dst_row, w1_all, w3_all, w2_all, layer):
    d = w1_all.shape[-2]
    s_rows = _slab_rows(d)
    t = hn.shape[0] // s_rows
    tm = MOE_TM
    n_tiles = tile_expert.shape[0]
    f = w1_all.shape[-1]
    hbm = pl.BlockSpec(memory_space=pl.ANY)
    grid_spec = pltpu.PrefetchScalarGridSpec(
        num_scalar_prefetch=5,
        grid=(n_tiles,),
        in_specs=[hbm, hbm, hbm, hbm],
        out_specs=hbm,
        scratch_shapes=[pltpu.VMEM((tm * s_rows, LANES), F32)] * 4
                       + [pltpu.VMEM((d, f), F32), pltpu.VMEM((d, f), F32), pltpu.VMEM((f, d), F32),
                          pltpu.VMEM((d, f), BF16), pltpu.VMEM((d, f), BF16), pltpu.VMEM((f, d), BF16),
                          pltpu.SemaphoreType.DMA((2,)), pltpu.SemaphoreType.DMA((2,)),
                          pltpu.SemaphoreType.DMA((4,))],
    )
    return pl.pallas_call(
        functools.partial(_moe_body, tm=tm, d=d, layer=layer),
        grid_spec=grid_spec,
        out_shape=jax.ShapeDtypeStruct(((2 * t + tm) * s_rows, LANES), F32),
        compiler_params=_params("arbitrary"),
        name="moe_experts",
    )(tile_expert, next_expert, n_used, src_tok, dst_row, hn, w1_all, w3_all, w2_all)


def _moe_schedule(info, counts, t):
    tm = MOE_TM
    n_tiles = (2 * t) // tm + N_EXPERTS
    p = n_tiles * tm
    e_id = info[:, INFO_EXPERT:INFO_EXPERT + 2].astype(jnp.int32)
    rank = info[:, INFO_RANK:INFO_RANK + 2].astype(jnp.int32)
    cnt = counts[0, :N_EXPERTS].astype(jnp.int32)
    tiles_e = (cnt + tm - 1) // tm
    tile_end = jnp.cumsum(tiles_e)
    tile_start = tile_end - tiles_e
    n_used = tile_end[-1]
    pos = (tile_start * tm)[e_id] + rank
    tok = jnp.broadcast_to(jnp.arange(t, dtype=jnp.int32)[:, None], (t, 2))
    dst = tok + jnp.array([0, t], jnp.int32)[None, :]
    trash = 2 * t + jnp.arange(p, dtype=jnp.int32) % tm
    default = jnp.stack([jnp.zeros((p,), jnp.int32), trash], axis=1)
    update = jnp.stack([tok.reshape(-1), dst.reshape(-1)], axis=1)
    table = default.at[pos.reshape(-1)].set(update)
    src_tok = table[:, 0]
    dst_row = jnp.concatenate([trash[:tm], table[:, 1]])
    tile_ids = jnp.minimum(jnp.arange(n_tiles, dtype=jnp.int32), n_used - 1)
    tile_expert = jnp.sum(tile_ids[:, None] >= tile_end[None, :], axis=1).astype(jnp.int32)
    after = tile_end[tile_expert]
    next_expert = jnp.where(after < n_used, tile_expert[jnp.minimum(after, n_tiles - 1)], -1).astype(jnp.int32)
    return tile_expert, next_expert, n_used.reshape(1).astype(jnp.int32), src_tok, dst_row


def _final_body(h_ref, y0_ref, y1_ref, info_ref, g_ref, o_ref):
    o_ref[...] = _rms(_moe_combine(h_ref, y0_ref, y1_ref, info_ref), g_ref[...])


def _final_norm(h, y, info, g):
    t, d = h.shape
    tm = min(FINAL_TM, t)
    nblk = t // tm
    return pl.pallas_call(
        _final_body,
        grid=(nblk,),
        in_specs=[pl.BlockSpec((tm, d), lambda i: (i, 0)),
                  pl.BlockSpec((tm * _slab_rows(d), LANES), lambda i: (i, 0)),
                  pl.BlockSpec((tm * _slab_rows(d), LANES), lambda i: (i + nblk, 0)),
                  pl.BlockSpec((tm, LANES), lambda i: (i, 0)),
                  pl.BlockSpec((1, d), lambda i: (0, 0))],
        out_specs=pl.BlockSpec((tm, d), lambda i: (i, 0)),
        out_shape=jax.ShapeDtypeStruct((t, d), F32),
        compiler_params=_params("arbitrary"),
        name="final_norm",
    )(h, y, y, info, g)


def kernel(x, norm_mix_g, w_in, sb_norm_g, ssm_lam_re, ssm_lam_im, ssm_b_re, ssm_b_im, ssm_c_re, ssm_c_im, ssm_d, ssm_log_dt, ssm_w_glu, ssm_b_glu, ssm_norm_g, diff_lq1, diff_lk1, diff_lq2, diff_lk2, diff_subln_g, w_out, norm_ffn_g, router_group_w, router_group_b, router_expert_w, router_expert_b, expert_w1, expert_w3, expert_w2, final_norm_g):
    bsz, seq, d = x.shape
    depth = w_in.shape[0]
    t = bsz * seq
    sbw = sb_norm_g.shape[-1]
    ssw = ssm_norm_g.shape[-1]
    dfw = d - sbw - ssw
    ssm_col = 3 * sbw
    diff_col = ssm_col + ssw

    w_in_b = w_in.astype(BF16)
    w_out_b = w_out.astype(BF16)
    w_glu_b = ssm_w_glu.astype(BF16)
    row = lambda v: v.reshape(1, -1).astype(F32)

    h = x.reshape(t, d)
    moe_out = None
    for l in range(depth):
        lam_init = 0.8 - 0.6 * math.exp(-0.3 * l)
        proj, ussm, xres = _norm_inproj(h, moe_out, row(norm_mix_g[l]), w_in_b, l, ssm_col, ssw)
        ysb = _sb_attention(proj, bsz, seq, sbw)
        ydf = _diff_attention(proj, [row(p[l]) for p in (diff_lq1, diff_lk1, diff_lq2, diff_lk2)],
                              row(diff_subln_g[l]), bsz, seq, diff_col, dfw, lam_init)
        s5_ops = _s5_operators(ssm_lam_re[l], ssm_lam_im[l], ssm_b_re[l], ssm_b_im[l], ssm_c_re[l], ssm_c_im[l],
                               ssm_d[l], ssm_log_dt[l], S5_CHUNK)
        yssm = _s5_scan(ussm, s5_ops, bsz, seq)

        wr = jnp.zeros((d, LANES), F32)
        wr = wr.at[:, :N_GROUPS].set(router_group_w[l])
        wr = wr.at[:, N_GROUPS:N_GROUPS + N_EXPERTS].set(
            router_expert_w[l].transpose(1, 0, 2).reshape(d, N_EXPERTS))
        wr_hi, wr_lo = _split_bf16(wr)
        rb = jnp.zeros((1, LANES), F32)
        rb = rb.at[0, :N_GROUPS].set(router_group_b[l])
        rb = rb.at[0, N_GROUPS:N_GROUPS + N_EXPERTS].set(router_expert_b[l].reshape(-1))

        h, hn, info, counts = _outproj_router(
            ysb, yssm, ydf, xres, row(sb_norm_g[l]), w_glu_b[l], row(ssm_b_glu[l]), row(ssm_norm_g[l]),
            w_out_b, row(norm_ffn_g[l]), wr_hi, wr_lo, rb, l)
        sched = _moe_schedule(info, counts, t)
        moe_out = (_moe_experts(hn, *sched, expert_w1, expert_w3, expert_w2, l), info)
    out = _final_norm(h, *moe_out, row(final_norm_g))
    return out.reshape(bsz, seq, d)
```

```python
import functools
import math

import numpy as np
import jax
import jax.numpy as jnp
from jax import lax
from jax.experimental import pallas as pl
from jax.experimental.pallas import tpu as pltpu

F32 = jnp.float32
BF16 = jnp.bfloat16

EPS = 1e-6
SB_HEAD_DIM = 64
SSM_GROUP_CH = 16
SSM_STATE = 64
DIFF_HEAD_DIM = 64
N_GROUPS = 4
EXPERTS_PER_GROUP = 8
N_EXPERTS = N_GROUPS * EXPERTS_PER_GROUP

LANES = 128
INFO_EXPERT, INFO_GATE, INFO_RANK = 0, 2, 4
VMEM_LIMIT = 56 * 1024 * 1024

INPROJ_TM = 512
INPROJ_TN = 1024
SB_TQ = 512
SB_TK = 256
DIFF_T = 512
S5_CHUNK = 8
OUT_TM = 256
MOE_TM = 256
FINAL_TM = 256


def _params(*sem):
    return pltpu.CompilerParams(dimension_semantics=sem, vmem_limit_bytes=VMEM_LIMIT)


def _dot(a, b):
    return jnp.dot(a, b, preferred_element_type=F32)


def _dot_nt(a, b):
    return lax.dot_general(a, b, (((1,), (1,)), ((), ())), preferred_element_type=F32)


def _slab_rows(d):
    return d // LANES


def _load_slabs(ref, n_tok, d):
    s_rows = _slab_rows(d)
    return jnp.concatenate([ref[pl.ds(s, n_tok, stride=s_rows), :] for s in range(s_rows)], axis=1)


def _store_slabs(ref, x):
    n_tok, d = x.shape
    s_rows = _slab_rows(d)
    for s in range(s_rows):
        ref[pl.ds(s, n_tok, stride=s_rows), :] = x[:, s * LANES:(s + 1) * LANES]


def _load_wide(ref, rows=slice(None)):
    return jnp.concatenate([ref[c, rows, :] for c in range(ref.shape[0])], axis=1)


def _store_wide(ref, x, rows=slice(None)):
    for c in range(ref.shape[0]):
        ref[c, rows, :] = x[:, c * LANES:(c + 1) * LANES]


def _split_bf16(x):
    hi = x.astype(BF16)
    lo = (x - hi.astype(F32)).astype(BF16)
    return hi, lo


def _moe_combine(h_ref, y0_ref, y1_ref, info_ref):
    info = info_ref[...]
    n_tok, d = h_ref.shape
    return (h_ref[...] + info[:, INFO_GATE:INFO_GATE + 1] * _load_slabs(y0_ref, n_tok, d)
            + info[:, INFO_GATE + 1:INFO_GATE + 2] * _load_slabs(y1_ref, n_tok, d))


def _inproj_body(*refs, after_moe, ssm_blk, ssm_off, ssm_w, chunk):
    n_in = 4 if after_moe else 1
    g_ref, w_ref = refs[n_in:n_in + 2]
    if after_moe:
        proj_ref, ussm_ref, x_out_ref, xn_ref, u_scr = refs[n_in + 2:]
    else:
        proj_ref, ussm_ref, xn_ref, u_scr = refs[n_in + 2:]

    @pl.when(pl.program_id(1) == 0)
    def _():
        if after_moe:
            x = _moe_combine(*refs[:4])
            x_out_ref[...] = x
        else:
            x = refs[0][...]
        ms = jnp.mean(x * x, axis=-1, keepdims=True)
        xn_ref[...] = (x * lax.rsqrt(ms + EPS) * g_ref[...]).astype(BF16)

    res = _dot(xn_ref[...], w_ref[...])
    proj_ref[...] = res.astype(BF16)

    @pl.when(pl.program_id(1) == ssm_blk)
    def _():
        _store_wide(u_scr, res[:, ssm_off:ssm_off + ssm_w])
        n_chunks = u_scr.shape[1] // chunk
        for t in range(chunk):
            ussm_ref[t] = _load_wide(u_scr, pl.ds(t, n_chunks, stride=chunk)).astype(BF16)


def _norm_inproj(x, moe_out, g, w_all, layer, ssm_col, ssm_w):
    t, d = x.shape
    n = w_all.shape[-1]
    tm, tn = min(INPROJ_TM, t), INPROJ_TN
    chunk = S5_CHUNK
    ssm_blk, ssm_off = divmod(ssm_col, tn)
    assert ssm_off + ssm_w <= tn and tm % chunk == 0
    rows = lambda blk, w: pl.BlockSpec((tm, w), lambda i, j: (i + blk, 0))
    args, in_specs = [x], [rows(0, d)]
    if moe_out is not None:
        y, info = moe_out
        slabs = lambda blk: pl.BlockSpec((tm * _slab_rows(d), LANES), lambda i, j: (i + blk, 0))
        args += [y, y, info]
        in_specs += [slabs(0), slabs(t // tm), rows(0, LANES)]
    in_specs += [pl.BlockSpec((1, d), lambda i, j: (0, 0)),
                 pl.BlockSpec((None, d, tn), lambda i, j: (layer, 0, j))]
    out_shape = [jax.ShapeDtypeStruct((t, n), BF16), jax.ShapeDtypeStruct((chunk, t // chunk, ssm_w), BF16)]
    out_specs = [pl.BlockSpec((tm, tn), lambda i, j: (i, j)),
                 pl.BlockSpec((chunk, tm // chunk, ssm_w), lambda i, j: (0, i, 0))]
    if moe_out is not None:
        out_shape.append(jax.ShapeDtypeStruct((t, d), F32))
        out_specs.append(pl.BlockSpec((tm, d), lambda i, j: (i, 0)))
    outs = pl.pallas_call(
        functools.partial(_inproj_body, after_moe=moe_out is not None, ssm_blk=ssm_blk, ssm_off=ssm_off,
                          ssm_w=ssm_w, chunk=chunk),
        grid=(t // tm, n // tn),
        in_specs=in_specs, out_specs=out_specs, out_shape=out_shape,
        scratch_shapes=[pltpu.VMEM((tm, d), BF16), pltpu.VMEM((ssm_w // LANES, tm, LANES), F32)],
        compiler_params=_params("arbitrary", "arbitrary"),
        name="norm_inproj",
    )(*args, g, w_all)
    return outs if moe_out is not None else (outs[0], outs[1], x)


def _sb_body(q_ref, k_ref, v_ref, o_ref, *, tq, tk):
    qi = pl.program_id(2)
    hd = SB_HEAD_DIM
    nsub = tq // tk
    lane = lax.broadcasted_iota(jnp.int32, (1, 2 * hd), 1)
    head_lanes = (lane < hd, lane >= hd)
    qs = q_ref[...] * (hd ** -0.5)
    zero = jnp.zeros((), BF16)
    qm = [jnp.where(m, qs, zero) for m in head_lanes]
    later = (lax.broadcasted_iota(jnp.int32, (tk, tk), 0)
             > lax.broadcasted_iota(jnp.int32, (tk, tk), 1)).astype(BF16)
    q_pos = qi * tq + lax.broadcasted_iota(jnp.int32, (tq, tk), 0)
    k_off = lax.broadcasted_iota(jnp.int32, (tq, tk), 1)

    def block(kb, carry, diag, r0=0):
        acc, runs = carry
        start = pl.multiple_of(kb * tk, tk)
        kblk = k_ref[pl.ds(start, tk), :]
        vblk = v_ref[pl.ds(start, tk), :]
        if diag:
            strict = (kb * tk + k_off < q_pos)[r0:]
        new_runs = []
        for h in range(2):
            z = _dot_nt(qm[h][r0:], kblk)
            log_fail = -jnp.maximum(z, 0.0) - jnp.log(1.0 + jnp.exp(-jnp.abs(z)))
            log_hit = log_fail + z
            if diag:
                log_fail = jnp.where(strict, log_fail, 0.0)
            log_after = _dot(log_fail.astype(BF16), later)
            w = jnp.exp(log_hit + log_after + runs[h][r0:])
            if diag:
                w = jnp.where(strict, w, 0.0)
            vm = jnp.where(head_lanes[h], vblk, zero)
            upd = _dot(w.astype(BF16), vm)
            run = runs[h][r0:] + jnp.sum(log_fail, axis=-1, keepdims=True)
            if r0:
                acc = jnp.concatenate([acc[:r0], acc[r0:] + upd], axis=0)
                run = jnp.concatenate([runs[h][:r0], run], axis=0)
            else:
                acc = acc + upd
            new_runs.append(run)
        return acc, tuple(new_runs)

    zrun = jnp.zeros((tq, 1), F32)
    carry = (jnp.zeros((tq, 2 * hd), F32), (zrun, zrun))
    for sub in reversed(range(nsub)):
        carry = block(nsub * qi + sub, carry, True, r0=sub * tk)
    carry = lax.fori_loop(0, nsub * qi, lambda it, c: block(nsub * qi - 1 - it, c, False), carry)
    o_ref[...] = carry[0].astype(BF16)


def _sb_attention(proj, bsz, seq, width):
    tq, tk = min(SB_TQ, seq), min(SB_TK, seq)
    nq = seq // tq
    npair = width // LANES
    kernel = functools.partial(_sb_body, tq=tq, tk=tk)
    return pl.pallas_call(
        kernel,
        grid=(bsz, npair, nq),
        in_specs=[pl.BlockSpec((tq, LANES), lambda b, p, i: (b * nq + i, p)),
                  pl.BlockSpec((seq, LANES), lambda b, p, i: (b, npair + p)),
                  pl.BlockSpec((seq, LANES), lambda b, p, i: (b, 2 * npair + p))],
        out_specs=pl.BlockSpec((tq, LANES), lambda b, p, i: (b * nq + i, p)),
        out_shape=jax.ShapeDtypeStruct((bsz * seq, width), BF16),
        compiler_params=_params("arbitrary", "arbitrary", "arbitrary"),
        name="sb_attention",
    )(proj, proj, proj)


def _diff_body(lq1_ref, lk1_ref, lq2_ref, lk2_ref, g_ref, q_ref, k_ref, v_ref, o_ref, s_scr, *, t, lam_init):
    qi = pl.program_id(2)
    hd = DIFF_HEAD_DIM
    lam = (jnp.exp(jnp.sum(lq1_ref[...] * lk1_ref[...], axis=-1, keepdims=True))
           - jnp.exp(jnp.sum(lq2_ref[...] * lk2_ref[...], axis=-1, keepdims=True)) + lam_init)
    lane = lax.broadcasted_iota(jnp.int32, (1, 2 * hd), 1)
    qs = q_ref[...] * (hd ** -0.5)
    zero = jnp.zeros((), BF16)
    qm = [jnp.where(lane < hd, qs, zero), jnp.where(lane >= hd, qs, zero)]
    row = lax.broadcasted_iota(jnp.int32, (t, t), 0)
    col = lax.broadcasted_iota(jnp.int32, (t, t), 1)
    causal = col <= row

    hs = t // 2

    def lane_max(acc, s):
        for c in range(s.shape[1] // LANES):
            acc = jnp.maximum(acc, s[:, c * LANES:(c + 1) * LANES])
        return acc

    def score_block(kb, mx):
        kblk = k_ref[pl.ds(pl.multiple_of(kb * t, t), t), :]
        out = []
        for h in range(2):
            s = _dot_nt(qm[h], kblk)
            s_scr[h, kb] = s
            out.append(lane_max(mx[h], s))
        return tuple(out)

    def diag_scores():
        kblk = k_ref[pl.ds(pl.multiple_of(qi * t, t), t), :]
        out = []
        for h in range(2):
            s_left = jnp.where(causal[:, :hs], _dot_nt(qm[h], kblk[:hs]), -jnp.inf)
            s_right = jnp.where(causal[hs:, hs:], _dot_nt(qm[h][hs:], kblk[hs:]), -jnp.inf)
            s_scr[h, qi, :, pl.ds(0, hs)] = s_left
            s_scr[h, qi, pl.ds(hs, hs), pl.ds(hs, hs)] = s_right
            smax = lane_max(jnp.full((t, LANES), -jnp.inf, F32), s_left)
            out.append(jnp.concatenate([smax[:hs], lane_max(smax[hs:], s_right)], axis=0))
        return tuple(out)

    mx = lax.fori_loop(0, qi, score_block, diag_scores())
    m = [jnp.broadcast_to(jnp.max(mx[h], axis=-1, keepdims=True), (t, t)) for h in range(2)]
    ones = jnp.ones((t, LANES), BF16)

    def pv_block(kb, acc):
        vaug = jnp.concatenate([v_ref[pl.ds(pl.multiple_of(kb * t, t), t), :], ones], axis=1)
        return tuple(acc[h] + _dot(jnp.exp(s_scr[h, kb] - m[h]).astype(BF16), vaug) for h in range(2))

    def pv_diag(acc):
        vaug = jnp.concatenate([v_ref[pl.ds(pl.multiple_of(qi * t, t), t), :], ones], axis=1)
        out = []
        for h in range(2):
            p_left = jnp.exp(s_scr[h, qi, :, pl.ds(0, hs)] - m[h][:, :hs]).astype(BF16)
            p_right = jnp.exp(s_scr[h, qi, pl.ds(hs, hs), pl.ds(hs, hs)] - m[h][hs:, :hs]).astype(BF16)
            a = acc[h] + _dot(p_left, vaug[:hs])
            out.append(jnp.concatenate([a[:hs], a[hs:] + _dot(p_right, vaug[hs:])], axis=0))
        return tuple(out)

    acc0 = jnp.zeros((t, 2 * LANES), F32)
    a0, a1 = pv_diag(lax.fori_loop(0, qi, pv_block, (acc0, acc0)))
    o = a0[:, :LANES] / a0[:, LANES:LANES + 1] - lam * (a1[:, :LANES] / a1[:, LANES:LANES + 1])
    ms = jnp.mean(o * o, axis=-1, keepdims=True)
    o_ref[...] = (o * lax.rsqrt(ms + EPS) * g_ref[...] * (1.0 - lam_init)).astype(BF16)


def _diff_attention(proj, lam_params, g, bsz, seq, col0, width, lam_init):
    t = min(DIFF_T, seq)
    nq = seq // t
    nh = width // LANES
    c0 = col0 // LANES
    small = pl.BlockSpec((1, DIFF_HEAD_DIM), lambda b, h, i: (0, 0))
    kernel = functools.partial(_diff_body, t=t, lam_init=lam_init)
    return pl.pallas_call(
        kernel,
        grid=(bsz, nh, nq),
        in_specs=[small, small, small, small,
                  pl.BlockSpec((1, LANES), lambda b, h, i: (0, 0)),
                  pl.BlockSpec((t, LANES), lambda b, h, i: (b * nq + i, c0 + h)),
                  pl.BlockSpec((seq, LANES), lambda b, h, i: (b, c0 + nh + h)),
                  pl.BlockSpec((seq, LANES), lambda b, h, i: (b, c0 + 2 * nh + h))],
        out_specs=pl.BlockSpec((t, LANES), lambda b, h, i: (b * nq + i, h)),
        out_shape=jax.ShapeDtypeStruct((bsz * seq, width), BF16),
        scratch_shapes=[pltpu.VMEM((2, nq, t, t), F32)],
        compiler_params=_params("arbitrary", "arbitrary", "arbitrary"),
        name="diff_attention",
    )(*lam_params, g, proj, proj, proj)


def _s5_body(u_ref, m_ref, ere_ref, eim_ref, fre_ref, fim_ref, are_ref, aim_ref, y_ref,
             xre_s, xim_s, sre_s, sim_s, *, bsz, nchunk, chunk):
    u = jnp.concatenate([u_ref[t] for t in range(chunk)], axis=1)
    _store_wide(xre_s, _dot(u, ere_ref[...]))
    _store_wide(xim_s, _dot(u, eim_ref[...]))
    are = are_ref[...]
    aim = aim_ref[...]
    sre = jnp.zeros((bsz, are.shape[1]), F32)
    sim = jnp.zeros((bsz, are.shape[1]), F32)
    for j in range(nchunk):
        rows = pl.ds(j, bsz, stride=nchunk)
        _store_wide(sre_s, sre, rows)
        _store_wide(sim_s, sim, rows)
        sre, sim = (are * sre - aim * sim + _load_wide(xre_s, rows),
                    are * sim + aim * sre + _load_wide(xim_s, rows))
    y = (_dot(u, m_ref[...])
         + _dot(_load_wide(sre_s).astype(BF16), fre_ref[...])
         + _dot(_load_wide(sim_s).astype(BF16), fim_ref[...]))
    for t in range(chunk):
        y_ref[t] = y[:, t * LANES:(t + 1) * LANES]


def _s5_operators(lam_re, lam_im, b_re, b_im, c_re, c_im, d_skip, log_dt, chunk):
    hp = lax.Precision.HIGHEST
    g, n = lam_re.shape
    ch = b_re.shape[-1]
    dt = jnp.exp(log_dt.astype(F32))[:, None]
    k = jnp.arange(chunk + 1, dtype=F32)[None, :, None]
    mag = jnp.exp(k * (lam_re * dt)[:, None, :])
    ang = k * (lam_im * dt)[:, None, :]
    p_re, p_im = mag * jnp.cos(ang), mag * jnp.sin(ang)
    lb_re, lb_im = p_re[:, 1], p_im[:, 1]
    den = lam_re * lam_re + lam_im * lam_im
    q_re = ((lb_re - 1.0) * lam_re + lb_im * lam_im) / den
    q_im = (lb_im * lam_re - (lb_re - 1.0) * lam_im) / den
    bb_re = q_re[:, :, None] * b_re - q_im[:, :, None] * b_im
    bb_im = q_re[:, :, None] * b_im + q_im[:, :, None] * b_re
    pb_re = p_re[:, :, :, None] * bb_re[:, None] - p_im[:, :, :, None] * bb_im[:, None]
    pb_im = p_re[:, :, :, None] * bb_im[:, None] + p_im[:, :, :, None] * bb_re[:, None]
    w = (jnp.einsum('gon,gkni->gkio', c_re, pb_re[:, :chunk], precision=hp)
         - jnp.einsum('gon,gkni->gkio', c_im, pb_im[:, :chunk], precision=hp))
    w = w.at[:, 0].add(jnp.eye(ch, dtype=F32)[None] * d_skip[:, :, None])
    gpt = LANES // ch
    nt = g // gpt

    def block_diag(x, rdim, cdim):
        rep = jnp.asarray(np.tile(np.eye(cdim, dtype=np.float32), (1, gpt)))
        mask = jnp.asarray((np.arange(gpt * rdim)[:, None] // rdim == np.arange(gpt * cdim)[None, :] // cdim)
                           .astype(np.float32))
        return jnp.einsum('akrc,cl->akrl', x, rep, precision=hp) * mask

    def per_tile(x, perm):
        x = x.reshape(nt, gpt, x.shape[1], x.shape[2], x.shape[3])
        x = x.transpose(0, 2, 1, 4, 3) if perm else x.transpose(0, 2, 1, 3, 4)
        return x.reshape(nt, x.shape[1], gpt * x.shape[3], x.shape[4])

    bd = block_diag(per_tile(w, False), ch, ch).astype(BF16)
    zero = jnp.zeros_like(bd[:, 0])
    m = jnp.concatenate([jnp.concatenate([bd[:, t - s] if t >= s else zero for t in range(chunk)], axis=2)
                         for s in range(chunk)], axis=1)
    to_state = lambda pb: block_diag(per_tile(pb[:, :chunk][:, ::-1], True), ch, n).astype(BF16).reshape(
        nt, chunk * LANES, gpt * n)
    e_re, e_im = to_state(pb_re), to_state(pb_im)
    cp_re = c_re[:, None] * p_re[:, 1:, None, :] - c_im[:, None] * p_im[:, 1:, None, :]
    cp_im = c_re[:, None] * p_im[:, 1:, None, :] + c_im[:, None] * p_re[:, 1:, None, :]

    def from_state(cp):
        blocks = block_diag(per_tile(cp, True), n, ch).astype(BF16)
        return jnp.concatenate([blocks[:, t] for t in range(chunk)], axis=2)

    f_re, f_im = from_state(cp_re), from_state(-cp_im)
    a_re = p_re[:, chunk].reshape(nt, 1, gpt * n)
    a_im = p_im[:, chunk].reshape(nt, 1, gpt * n)
    return m, e_re, e_im, f_re, f_im, a_re, a_im


def _s5_scan(u, ops, bsz, seq):
    chunk, rows, width = u.shape
    nchunk = seq // chunk
    nt = width // LANES
    kdim = chunk * LANES
    sdim = ops[1].shape[-1]
    op = lambda a, b: pl.BlockSpec((None, a, b), lambda i: (i, 0, 0))
    act = pl.BlockSpec((chunk, rows, LANES), lambda i: (0, 0, i))
    return pl.pallas_call(
        functools.partial(_s5_body, bsz=bsz, nchunk=nchunk, chunk=chunk),
        grid=(nt,),
        in_specs=[act, op(kdim, kdim), op(kdim, sdim), op(kdim, sdim), op(sdim, kdim), op(sdim, kdim),
                  op(1, sdim), op(1, sdim)],
        out_specs=act,
        out_shape=jax.ShapeDtypeStruct((chunk, rows, width), F32),
        scratch_shapes=[pltpu.VMEM((sdim // LANES, rows, LANES), F32)] * 4,
        compiler_params=_params("arbitrary"),
        name="s5_scan",
    )(u, *ops)


def _rms(x, g):
    ms = jnp.mean(x * x, axis=-1, keepdims=True)
    return x * lax.rsqrt(ms + EPS) * g


def _outproj_body(ysb_ref, yssm_ref, ydf_ref, x_ref, sbg_ref, wglu_ref, bglu_ref, ssmg_ref, wout_ref,
                  ffng_ref, wrhi_ref, wrlo_ref, rb_ref,
                  h_ref, hn_ref, info_ref, cnt_ref, run_s, yssm_s, *, tm, sbw, ssw):
    step = pl.program_id(0)

    @pl.when(step == 0)
    def _():
        run_s[...] = jnp.zeros_like(run_s)

    ysb = _rms(ysb_ref[...].astype(F32), sbg_ref[...]).astype(BF16)
    chunk = yssm_ref.shape[0]
    for t in range(chunk):
        _store_wide(yssm_s, yssm_ref[t], pl.ds(t, tm // chunk, stride=chunk))
    y = jax.nn.gelu(_load_wide(yssm_s))
    y = y * jax.nn.sigmoid(_dot(y.astype(BF16), wglu_ref[...]) + bglu_ref[...])
    yssm = _rms(y, ssmg_ref[...]).astype(BF16)
    h = (x_ref[...]
         + _dot(ysb, wout_ref[0:sbw, :])
         + _dot(yssm, wout_ref[sbw:sbw + ssw, :])
         + _dot(ydf_ref[...], wout_ref[sbw + ssw:, :]))
    h_ref[...] = h
    hn = _rms(h, ffng_ref[...])
    _store_slabs(hn_ref, hn)

    hi, lo = _split_bf16(hn)
    logits = (_dot(hi, wrhi_ref[...]) + _dot(hi, wrlo_ref[...]) + _dot(lo, wrhi_ref[...])) + rb_ref[...]
    lane = lax.broadcasted_iota(jnp.int32, (tm, LANES), 1).astype(F32)
    ninf = -jnp.inf

    def first_max(v):
        m = jnp.max(v, axis=-1, keepdims=True)
        idx = jnp.min(jnp.where(v == m, lane, float(LANES)), axis=-1, keepdims=True)
        return m, idx

    gl = jnp.where(lane < N_GROUPS, logits, ninf)
    gmax, gidx = first_max(gl)
    g_top = 1.0 / jnp.sum(jnp.exp(gl - gmax), axis=-1, keepdims=True)
    group_lo = N_GROUPS + EXPERTS_PER_GROUP * gidx
    in_group = (lane >= group_lo) & (lane < group_lo + EXPERTS_PER_GROUP)
    el = jnp.where(in_group, logits, ninf)
    m1, i1 = first_max(el)
    m2, i2 = first_max(jnp.where(lane == i1, ninf, el))
    r = jnp.exp(m2 - m1)
    w_a = g_top / (1.0 + r)
    w_b = g_top * r / (1.0 + r)
    e_a = i1 - N_GROUPS
    e_b = i2 - N_GROUPS

    oh_a = (lane == e_a).astype(F32)
    oh_b = (lane == e_b).astype(F32)
    cnt = oh_a + oh_b
    trow = lax.broadcasted_iota(jnp.int32, (tm, tm), 0)
    tcol = lax.broadcasted_iota(jnp.int32, (tm, tm), 1)
    before = (tcol < trow).astype(BF16)
    base = _dot(before, cnt.astype(BF16)) + run_s[...]
    rank_a = jnp.sum(oh_a * base, axis=-1, keepdims=True)
    rank_b = jnp.sum(oh_b * base, axis=-1, keepdims=True)
    run_s[...] = run_s[...] + jnp.sum(cnt, axis=0, keepdims=True)
    cnt_ref[...] = run_s[...]

    info = jnp.zeros((tm, LANES), F32)
    for k, val in ((INFO_EXPERT, e_a), (INFO_EXPERT + 1, e_b), (INFO_GATE, w_a), (INFO_GATE + 1, w_b),
                   (INFO_RANK, rank_a), (INFO_RANK + 1, rank_b)):
        info = jnp.where(lane == k, val, info)
    info_ref[...] = info


def _outproj_router(ysb, yssm, ydf, x, sbg, wglu, bglu, ssmg, wout_all, ffng, wr_hi, wr_lo, rb, layer):
    t, d = x.shape
    tm = min(OUT_TM, t)
    sbw, ssw, dfw = ysb.shape[1], yssm.shape[2], ydf.shape[1]
    chunk = yssm.shape[0]
    rowblk = lambda w: pl.BlockSpec((tm, w), lambda i: (i, 0))
    const = lambda *shape: pl.BlockSpec(shape, lambda i: (0,) * len(shape))
    kernel = functools.partial(_outproj_body, tm=tm, sbw=sbw, ssw=ssw)
    return pl.pallas_call(
        kernel,
        grid=(t // tm,),
        in_specs=[rowblk(sbw), pl.BlockSpec((chunk, tm // chunk, ssw), lambda i: (0, i, 0)), rowblk(dfw), rowblk(d),
                  const(1, sbw), const(ssw, ssw), const(1, ssw), const(1, ssw),
                  pl.BlockSpec((None, d, d), lambda i: (layer, 0, 0)),
                  const(1, d), const(d, LANES), const(d, LANES), const(1, LANES)],
        out_specs=[rowblk(d), pl.BlockSpec((tm * _slab_rows(d), LANES), lambda i: (i, 0)),
                   rowblk(LANES), const(1, LANES)],
        out_shape=[jax.ShapeDtypeStruct((t, d), F32), jax.ShapeDtypeStruct((t * _slab_rows(d), LANES), F32),
                   jax.ShapeDtypeStruct((t, LANES), F32), jax.ShapeDtypeStruct((1, LANES), F32)],
        scratch_shapes=[pltpu.VMEM((1, LANES), F32), pltpu.VMEM((ssw // LANES, tm, LANES), F32)],
        compiler_params=_params("arbitrary"),
        name="outproj_router",
    )(ysb, yssm, ydf, x, sbg, wglu, bglu, ssmg, wout_all, ffng, wr_hi, wr_lo, rb)


def _moe_body(te_ref, nx_ref, nu_ref, src_ref, dst_ref,
              hn_hbm, w1_hbm, w3_hbm, w2_hbm,
              y_hbm,
              xbuf0, xbuf1, ybuf0, ybuf1, w1s, w3s, w2s, w1b, w3b, w2b, gsem, ssem, wsem, *, tm, d, layer):
    i = pl.program_id(0)
    n_used = nu_ref[0]
    xbufs = (xbuf0, xbuf1)
    ybufs = (ybuf0, ybuf1)
    s_rows = _slab_rows(d)

    def weight_copies(e):
        half = w2s.shape[0] // 2
        parts = ((w1_hbm.at[layer, e], w1s, 0), (w3_hbm.at[layer, e], w3s, 1),
                 (w2_hbm.at[layer, e, pl.ds(0, half), :], w2s.at[pl.ds(0, half), :], 0),
                 (w2_hbm.at[layer, e, pl.ds(half, half), :], w2s.at[pl.ds(half, half), :], 1))
        return [(pltpu.make_async_copy(src, dst, wsem.at[k]), queue) for k, (src, dst, queue) in enumerate(parts)]

    def slab(ref, tok):
        return ref.at[pl.ds(pl.multiple_of(tok * s_rows, s_rows), s_rows), :]

    def gather_rows(tile, par, start):
        if not start:
            pltpu.make_async_copy(hn_hbm.at[pl.ds(0, tm * s_rows), :], xbufs[par], gsem.at[par]).wait()
            return
        base = tile * tm
        for r in range(tm):
            pltpu.make_async_copy(slab(hn_hbm, src_ref[base + r]), slab(xbufs[par], r), gsem.at[par]).start()

    def scatter_rows(tile, par, start):
        if not start:
            pltpu.make_async_copy(ybufs[par], y_hbm.at[pl.ds(0, tm * s_rows), :], ssem.at[par]).wait()
            return
        base = (tile + 1) * tm
        for r in range(tm):
            pltpu.make_async_copy(slab(ybufs[par], r), slab(y_hbm, dst_ref[base + r]), ssem.at[par]).start(priority=1)

    @pl.when(i == 0)
    def _():
        ybuf1[...] = jnp.zeros_like(ybuf1)
        for cp, queue in weight_copies(te_ref[0]):
            cp.start(priority=queue)
        gather_rows(0, 0, True)
        gather_rows(0, 0, False)

    new_expert = jnp.logical_or(i == 0, te_ref[i] != te_ref[jnp.maximum(i - 1, 0)])

    @pl.when(jnp.logical_and(i < n_used, new_expert))
    def _():
        for cp, _ in weight_copies(te_ref[i]):
            cp.wait()
        w1b[...] = w1s[...].astype(BF16)
        w3b[...] = w3s[...].astype(BF16)
        w2b[...] = w2s[...].astype(BF16)

        @pl.when(nx_ref[i] >= 0)
        def _():
            for cp, queue in weight_copies(nx_ref[i]):
                cp.start(priority=queue)

    for par in range(2):
        active = jnp.logical_and(i < n_used, i % 2 == par)
        nxt = jnp.minimum(i + 1, n_used - 1)

        @pl.when(active)
        def _(par=par, nxt=nxt):
            x = _load_slabs(xbufs[par], tm, d).astype(BF16)
            gather_rows(nxt, 1 - par, True)
            scatter_rows(i - 1, 1 - par, True)
            h1 = _dot(x, w1b[...])
            h3 = _dot(x, w3b[...])
            a = (h1 * jax.nn.sigmoid(h1) * h3).astype(BF16)
            _store_slabs(ybufs[par], _dot(a, w2b[...]))

        @pl.when(active)
        def _(par=par, nxt=nxt):
            scatter_rows(i - 1, 1 - par, False)
            gather_rows(nxt, 1 - par, False)

        @pl.when(jnp.logical_and(active, i == n_used - 1))
        def _(par=par):
            scatter_rows(i, par, True)
            scatter_rows(i, par, False)


def _moe_experts(hn, tile_expert, next_expert, n_used, src_tok, dst_row, w1_all, w3_all, w2_all, layer):
    d = w1_all.shape[-2]
    s_rows = _slab_rows(d)
    t = hn.shape[0] // s_rows
    tm = MOE_TM
    n_tiles = tile_expert.shape[0]
    f = w1_all.shape[-1]
    hbm = pl.BlockSpec(memory_space=pl.ANY)
    grid_spec = pltpu.PrefetchScalarGridSpec(
        num_scalar_prefetch=5,
        grid=(n_tiles,),
        in_specs=[hbm, hbm, hbm, hbm],
        out_specs=hbm,
        scratch_shapes=[pltpu.VMEM((tm * s_rows, LANES), F32)] * 4
                       + [pltpu.VMEM((d, f), F32), pltpu.VMEM((d, f), F32), pltpu.VMEM((f, d), F32),
                          pltpu.VMEM((d, f), BF16), pltpu.VMEM((d, f), BF16), pltpu.VMEM((f, d), BF16),
                          pltpu.SemaphoreType.DMA((2,)), pltpu.SemaphoreType.DMA((2,)),
                          pltpu.SemaphoreType.DMA((4,))],
    )
    return pl.pallas_call(
        functools.partial(_moe_body, tm=tm, d=d, layer=layer),
        grid_spec=grid_spec,
        out_shape=jax.ShapeDtypeStruct(((2 * t + tm) * s_rows, LANES), F32),
        compiler_params=_params("arbitrary"),
        name="moe_experts",
    )(tile_expert, next_expert, n_used, src_tok, dst_row, hn, w1_all, w3_all, w2_all)


def _moe_schedule(info, counts, t):
    tm = MOE_TM
    n_tiles = (2 * t) // tm + N_EXPERTS
    p = n_tiles * tm
    e_id = info[:, INFO_EXPERT:INFO_EXPERT + 2].astype(jnp.int32)
    rank = info[:, INFO_RANK:INFO_RANK + 2].astype(jnp.int32)
    cnt = counts[0, :N_EXPERTS].astype(jnp.int32)
    tiles_e = (cnt + tm - 1) // tm
    tile_end = jnp.cumsum(tiles_e)
    tile_start = tile_end - tiles_e
    n_used = tile_end[-1]
    pos = (tile_start * tm)[e_id] + rank
    tok = jnp.broadcast_to(jnp.arange(t, dtype=jnp.int32)[:, None], (t, 2))
    dst = tok + jnp.array([0, t], jnp.int32)[None, :]
    trash = 2 * t + jnp.arange(p, dtype=jnp.int32) % tm
    default = jnp.stack([jnp.zeros((p,), jnp.int32), trash], axis=1)
    update = jnp.stack([tok.reshape(-1), dst.reshape(-1)], axis=1)
    table = default.at[pos.reshape(-1)].set(update)
    src_tok = table[:, 0]
    dst_row = jnp.concatenate([trash[:tm], table[:, 1]])
    tile_ids = jnp.minimum(jnp.arange(n_tiles, dtype=jnp.int32), n_used - 1)
    tile_expert = jnp.sum(tile_ids[:, None] >= tile_end[None, :], axis=1).astype(jnp.int32)
    after = tile_end[tile_expert]
    next_expert = jnp.where(after < n_used, tile_expert[jnp.minimum(after, n_tiles - 1)], -1).astype(jnp.int32)
    return tile_expert, next_expert, n_used.reshape(1).astype(jnp.int32), src_tok, dst_row


def _final_body(h_ref, y0_ref, y1_ref, info_ref, g_ref, o_ref):
    o_ref[...] = _rms(_moe_combine(h_ref, y0_ref, y1_ref, info_ref), g_ref[...])


def _final_norm(h, y, info, g):
    t, d = h.shape
    tm = min(FINAL_TM, t)
    nblk = t // tm
    return pl.pallas_call(
        _final_body,
        grid=(nblk,),
        in_specs=[pl.BlockSpec((tm, d), lambda i: (i, 0)),
                  pl.BlockSpec((tm * _slab_rows(d), LANES), lambda i: (i, 0)),
                  pl.BlockSpec((tm * _slab_rows(d), LANES), lambda i: (i + nblk, 0)),
                  pl.BlockSpec((tm, LANES), lambda i: (i, 0)),
                  pl.BlockSpec((1, d), lambda i: (0, 0))],
        out_specs=pl.BlockSpec((tm, d), lambda i: (i, 0)),
        out_shape=jax.ShapeDtypeStruct((t, d), F32),
        compiler_params=_params("arbitrary"),
        name="final_norm",
    )(h, y, y, info, g)


def kernel(x, norm_mix_g, w_in, sb_norm_g, ssm_lam_re, ssm_lam_im, ssm_b_re, ssm_b_im, ssm_c_re, ssm_c_im, ssm_d, ssm_log_dt, ssm_w_glu, ssm_b_glu, ssm_norm_g, diff_lq1, diff_lk1, diff_lq2, diff_lk2, diff_subln_g, w_out, norm_ffn_g, router_group_w, router_group_b, router_expert_w, router_expert_b, expert_w1, expert_w3, expert_w2, final_norm_g):
    bsz, seq, d = x.shape
    depth = w_in.shape[0]
    t = bsz * seq
    sbw = sb_norm_g.shape[-1]
    ssw = ssm_norm_g.shape[-1]
    dfw = d - sbw - ssw
    ssm_col = 3 * sbw
    diff_col = ssm_col + ssw

    w_in_b = w_in.astype(BF16)
    w_out_b = w_out.astype(BF16)
    w_glu_b = ssm_w_glu.astype(BF16)
    row = lambda v: v.reshape(1, -1).astype(F32)

    h = x.reshape(t, d)
    moe_out = None
    for l in range(depth):
        lam_init = 0.8 - 0.6 * math.exp(-0.3 * l)
        proj, ussm, xres = _norm_inproj(h, moe_out, row(norm_mix_g[l]), w_in_b, l, ssm_col, ssw)
        ysb = _sb_attention(proj, bsz, seq, sbw)
        ydf = _diff_attention(proj, [row(p[l]) for p in (diff_lq1, diff_lk1, diff_lq2, diff_lk2)],
                              row(diff_subln_g[l]), bsz, seq, diff_col, dfw, lam_init)
        s5_ops = _s5_operators(ssm_lam_re[l], ssm_lam_im[l], ssm_b_re[l], ssm_b_im[l], ssm_c_re[l], ssm_c_im[l],
                               ssm_d[l], ssm_log_dt[l], S5_CHUNK)
        yssm = _s5_scan(ussm, s5_ops, bsz, seq)

        wr = jnp.zeros((d, LANES), F32)
        wr = wr.at[:, :N_GROUPS].set(router_group_w[l])
        wr = wr.at[:, N_GROUPS:N_GROUPS + N_EXPERTS].set(
            router_expert_w[l].transpose(1, 0, 2).reshape(d, N_EXPERTS))
        wr_hi, wr_lo = _split_bf16(wr)
        rb = jnp.zeros((1, LANES), F32)
        rb = rb.at[0, :N_GROUPS].set(router_group_b[l])
        rb = rb.at[0, N_GROUPS:N_GROUPS + N_EXPERTS].set(router_expert_b[l].reshape(-1))

        h, hn, info, counts = _outproj_router(
            ysb, yssm, ydf, xres, row(sb_norm_g[l]), w_glu_b[l], row(ssm_b_glu[l]), row(ssm_norm_g[l]),
            w_out_b, row(norm_ffn_g[l]), wr_hi, wr_lo, rb, l)
        sched = _moe_schedule(info, counts, t)
        moe_out = (_moe_experts(hn, *sched, expert_w1, expert_w3, expert_w2, l), info)
    out = _final_norm(h, *moe_out, row(final_norm_g))
    return out.reshape(bsz, seq, d)
```

```python
import functools
import math

import numpy as np
import jax
import jax.numpy as jnp
from jax import lax
from jax.experimental import pallas as pl
from jax.experimental.pallas import tpu as pltpu

F32 = jnp.float32
BF16 = jnp.bfloat16

EPS = 1e-6
SB_HEAD_DIM = 64
SSM_GROUP_CH = 16
SSM_STATE = 64
DIFF_HEAD_DIM = 64
N_GROUPS = 4
EXPERTS_PER_GROUP = 8
N_EXPERTS = N_GROUPS * EXPERTS_PER_GROUP

LANES = 128
INFO_EXPERT, INFO_GATE, INFO_RANK = 0, 2, 4
VMEM_LIMIT = 56 * 1024 * 1024

INPROJ_TM = 512
INPROJ_TN = 1024
SB_TQ = 512
SB_TK = 256
DIFF_T = 512
S5_CHUNK = 8
OUT_TM = 256
MOE_TM = 256
FINAL_TM = 256


def _params(*sem):
    return pltpu.CompilerParams(dimension_semantics=sem, vmem_limit_bytes=VMEM_LIMIT)


def _dot(a, b):
    return jnp.dot(a, b, preferred_element_type=F32)


def _dot_nt(a, b):
    return lax.dot_general(a, b, (((1,), (1,)), ((), ())), preferred_element_type=F32)


def _slab_rows(d):
    return d // LANES


def _load_slabs(ref, n_tok, d):
    s_rows = _slab_rows(d)
    return jnp.concatenate([ref[pl.ds(s, n_tok, stride=s_rows), :] for s in range(s_rows)], axis=1)


def _store_slabs(ref, x):
    n_tok, d = x.shape
    s_rows = _slab_rows(d)
    for s in range(s_rows):
        ref[pl.ds(s, n_tok, stride=s_rows), :] = x[:, s * LANES:(s + 1) * LANES]


def _load_wide(ref, rows=slice(None)):
    return jnp.concatenate([ref[c, rows, :] for c in range(ref.shape[0])], axis=1)


def _store_wide(ref, x, rows=slice(None)):
    for c in range(ref.shape[0]):
        ref[c, rows, :] = x[:, c * LANES:(c + 1) * LANES]


def _split_bf16(x):
    hi = x.astype(BF16)
    lo = (x - hi.astype(F32)).astype(BF16)
    return hi, lo


def _moe_combine(h_ref, y0_ref, y1_ref, info_ref):
    info = info_ref[...]
    n_tok, d = h_ref.shape
    return (h_ref[...] + info[:, INFO_GATE:INFO_GATE + 1] * _load_slabs(y0_ref, n_tok, d)
            + info[:, INFO_GATE + 1:INFO_GATE + 2] * _load_slabs(y1_ref, n_tok, d))


def _inproj_body(*refs, after_moe, ssm_blk, ssm_off, ssm_w, chunk):
    n_in = 4 if after_moe else 1
    g_ref, w_ref = refs[n_in:n_in + 2]
    if after_moe:
        proj_ref, ussm_ref, x_out_ref, xn_ref, u_scr = refs[n_in + 2:]
    else:
        proj_ref, ussm_ref, xn_ref, u_scr = refs[n_in + 2:]

    @pl.when(pl.program_id(1) == 0)
    def _():
        if after_moe:
            x = _moe_combine(*refs[:4])
            x_out_ref[...] = x
        else:
            x = refs[0][...]
        ms = jnp.mean(x * x, axis=-1, keepdims=True)
        xn_ref[...] = (x * lax.rsqrt(ms + EPS) * g_ref[...]).astype(BF16)

    res = _dot(xn_ref[...], w_ref[...])
    proj_ref[...] = res.astype(BF16)

    @pl.when(pl.program_id(1) == ssm_blk)
    def _():
        _store_wide(u_scr, res[:, ssm_off:ssm_off + ssm_w])
        n_chunks = u_scr.shape[1] // chunk
        for t in range(chunk):
            ussm_ref[t] = _load_wide(u_scr, pl.ds(t, n_chunks, stride=chunk)).astype(BF16)


def _norm_inproj(x, moe_out, g, w_all, layer, ssm_col, ssm_w):
    t, d = x.shape
    n = w_all.shape[-1]
    tm, tn = min(INPROJ_TM, t), INPROJ_TN
    chunk = S5_CHUNK
    ssm_blk, ssm_off = divmod(ssm_col, tn)
    assert ssm_off + ssm_w <= tn and tm % chunk == 0
    rows = lambda blk, w: pl.BlockSpec((tm, w), lambda i, j: (i + blk, 0))
    args, in_specs = [x], [rows(0, d)]
    if moe_out is not None:
        y, info = moe_out
        slabs = lambda blk: pl.BlockSpec((tm * _slab_rows(d), LANES), lambda i, j: (i + blk, 0))
        args += [y, y, info]
        in_specs += [slabs(0), slabs(t // tm), rows(0, LANES)]
    in_specs += [pl.BlockSpec((1, d), lambda i, j: (0, 0)),
                 pl.BlockSpec((None, d, tn), lambda i, j: (layer, 0, j))]
    out_shape = [jax.ShapeDtypeStruct((t, n), BF16), jax.ShapeDtypeStruct((chunk, t // chunk, ssm_w), BF16)]
    out_specs = [pl.BlockSpec((tm, tn), lambda i, j: (i, j)),
                 pl.BlockSpec((chunk, tm // chunk, ssm_w), lambda i, j: (0, i, 0))]
    if moe_out is not None:
        out_shape.append(jax.ShapeDtypeStruct((t, d), F32))
        out_specs.append(pl.BlockSpec((tm, d), lambda i, j: (i, 0)))
    outs = pl.pallas_call(
        functools.partial(_inproj_body, after_moe=moe_out is not None, ssm_blk=ssm_blk, ssm_off=ssm_off,
                          ssm_w=ssm_w, chunk=chunk),
        grid=(t // tm, n // tn),
        in_specs=in_specs, out_specs=out_specs, out_shape=out_shape,
        scratch_shapes=[pltpu.VMEM((tm, d), BF16), pltpu.VMEM((ssm_w // LANES, tm, LANES), F32)],
        compiler_params=_params("arbitrary", "arbitrary"),
        name="norm_inproj",
    )(*args, g, w_all)
    return outs if moe_out is not None else (outs[0], outs[1], x)


def _sb_body(q_ref, k_ref, v_ref, o_ref, *, tq, tk):
    qi = pl.program_id(2)
    hd = SB_HEAD_DIM
    nsub = tq // tk
    lane = lax.broadcasted_iota(jnp.int32, (1, 2 * hd), 1)
    head_lanes = (lane < hd, lane >= hd)
    qs = q_ref[...] * (hd ** -0.5)
    zero = jnp.zeros((), BF16)
    qm = [jnp.where(m, qs, zero) for m in head_lanes]
    later = (lax.broadcasted_iota(jnp.int32, (tk, tk), 0)
             > lax.broadcasted_iota(jnp.int32, (tk, tk), 1)).astype(BF16)
    q_pos = qi * tq + lax.broadcasted_iota(jnp.int32, (tq, tk), 0)
    k_off = lax.broadcasted_iota(jnp.int32, (tq, tk), 1)

    def block(kb, carry, diag, r0=0):
        acc, runs = carry
        start = pl.multiple_of(kb * tk, tk)
        kblk = k_ref[pl.ds(start, tk), :]
        vblk = v_ref[pl.ds(start, tk), :]
        if diag:
            strict = (kb * tk + k_off < q_pos)[r0:]
        new_runs = []
        for h in range(2):
            z = _dot_nt(qm[h][r0:], kblk)
            log_fail = -jnp.maximum(z, 0.0) - jnp.log(1.0 + jnp.exp(-jnp.abs(z)))
            log_hit = log_fail + z
            if diag:
                log_fail = jnp.where(strict, log_fail, 0.0)
            log_after = _dot(log_fail.astype(BF16), later)
            w = jnp.exp(log_hit + log_after + runs[h][r0:])
            if diag:
                w = jnp.where(strict, w, 0.0)
            vm = jnp.where(head_lanes[h], vblk, zero)
            upd = _dot(w.astype(BF16), vm)
            run = runs[h][r0:] + jnp.sum(log_fail, axis=-1, keepdims=True)
            if r0:
                acc = jnp.concatenate([acc[:r0], acc[r0:] + upd], axis=0)
                run = jnp.concatenate([runs[h][:r0], run], axis=0)
            else:
                acc = acc + upd
            new_runs.append(run)
        return acc, tuple(new_runs)

    zrun = jnp.zeros((tq, 1), F32)
    carry = (jnp.zeros((tq, 2 * hd), F32), (zrun, zrun))
    for sub in reversed(range(nsub)):
        carry = block(nsub * qi + sub, carry, True, r0=sub * tk)
    def earlier_tile(it, c):
        for u in range(nsub):
            c = block(nsub * (qi - it) - 1 - u, c, False)
        return c

    carry = lax.fori_loop(0, qi, earlier_tile, carry)
    o_ref[...] = carry[0].astype(BF16)


def _sb_attention(proj, bsz, seq, width):
    tq, tk = min(SB_TQ, seq), min(SB_TK, seq)
    nq = seq // tq
    npair = width // LANES
    kernel = functools.partial(_sb_body, tq=tq, tk=tk)
    return pl.pallas_call(
        kernel,
        grid=(bsz, npair, nq),
        in_specs=[pl.BlockSpec((tq, LANES), lambda b, p, i: (b * nq + i, p)),
                  pl.BlockSpec((seq, LANES), lambda b, p, i: (b, npair + p)),
                  pl.BlockSpec((seq, LANES), lambda b, p, i: (b, 2 * npair + p))],
        out_specs=pl.BlockSpec((tq, LANES), lambda b, p, i: (b * nq + i, p)),
        out_shape=jax.ShapeDtypeStruct((bsz * seq, width), BF16),
        compiler_params=_params("arbitrary", "arbitrary", "arbitrary"),
        name="sb_attention",
    )(proj, proj, proj)


def _diff_body(lq1_ref, lk1_ref, lq2_ref, lk2_ref, g_ref, q_ref, k_ref, v_ref, o_ref, s_scr, *, t, lam_init):
    qi = pl.program_id(2)
    hd = DIFF_HEAD_DIM
    lam = (jnp.exp(jnp.sum(lq1_ref[...] * lk1_ref[...], axis=-1, keepdims=True))
           - jnp.exp(jnp.sum(lq2_ref[...] * lk2_ref[...], axis=-1, keepdims=True)) + lam_init)
    lane = lax.broadcasted_iota(jnp.int32, (1, 2 * hd), 1)
    qs = q_ref[...] * (hd ** -0.5)
    zero = jnp.zeros((), BF16)
    qm = [jnp.where(lane < hd, qs, zero), jnp.where(lane >= hd, qs, zero)]
    row = lax.broadcasted_iota(jnp.int32, (t, t), 0)
    col = lax.broadcasted_iota(jnp.int32, (t, t), 1)
    causal = col <= row

    hs = t // 2

    def lane_max(acc, s):
        for c in range(s.shape[1] // LANES):
            acc = jnp.maximum(acc, s[:, c * LANES:(c + 1) * LANES])
        return acc

    def score_block(kb, mx):
        kblk = k_ref[pl.ds(pl.multiple_of(kb * t, t), t), :]
        out = []
        for h in range(2):
            s = _dot_nt(qm[h], kblk)
            s_scr[h, kb] = s
            out.append(lane_max(mx[h], s))
        return tuple(out)

    def diag_scores():
        kblk = k_ref[pl.ds(pl.multiple_of(qi * t, t), t), :]
        out = []
        for h in range(2):
            s_left = jnp.where(causal[:, :hs], _dot_nt(qm[h], kblk[:hs]), -jnp.inf)
            s_right = jnp.where(causal[hs:, hs:], _dot_nt(qm[h][hs:], kblk[hs:]), -jnp.inf)
            s_scr[h, qi, :, pl.ds(0, hs)] = s_left
            s_scr[h, qi, pl.ds(hs, hs), pl.ds(hs, hs)] = s_right
            smax = lane_max(jnp.full((t, LANES), -jnp.inf, F32), s_left)
            out.append(jnp.concatenate([smax[:hs], lane_max(smax[hs:], s_right)], axis=0))
        return tuple(out)

    mx = lax.fori_loop(0, qi, score_block, diag_scores())
    m = [jnp.broadcast_to(jnp.max(mx[h], axis=-1, keepdims=True), (t, t)) for h in range(2)]
    ones = jnp.ones((t, LANES), BF16)

    def pv_block(kb, acc):
        vaug = jnp.concatenate([v_ref[pl.ds(pl.multiple_of(kb * t, t), t), :], ones], axis=1)
        return tuple(acc[h] + _dot(jnp.exp(s_scr[h, kb] - m[h]).astype(BF16), vaug) for h in range(2))

    def pv_diag(acc):
        vaug = jnp.concatenate([v_ref[pl.ds(pl.multiple_of(qi * t, t), t), :], ones], axis=1)
        out = []
        for h in range(2):
            p_left = jnp.exp(s_scr[h, qi, :, pl.ds(0, hs)] - m[h][:, :hs]).astype(BF16)
            p_right = jnp.exp(s_scr[h, qi, pl.ds(hs, hs), pl.ds(hs, hs)] - m[h][hs:, :hs]).astype(BF16)
            a = acc[h] + _dot(p_left, vaug[:hs])
            out.append(jnp.concatenate([a[:hs], a[hs:] + _dot(p_right, vaug[hs:])], axis=0))
        return tuple(out)

    acc0 = jnp.zeros((t, 2 * LANES), F32)
    a0, a1 = pv_diag(lax.fori_loop(0, qi, pv_block, (acc0, acc0)))
    o = a0[:, :LANES] / a0[:, LANES:LANES + 1] - lam * (a1[:, :LANES] / a1[:, LANES:LANES + 1])
    ms = jnp.mean(o * o, axis=-1, keepdims=True)
    o_ref[...] = (o * lax.rsqrt(ms + EPS) * g_ref[...] * (1.0 - lam_init)).astype(BF16)


def _diff_attention(proj, lam_params, g, bsz, seq, col0, width, lam_init):
    t = min(DIFF_T, seq)
    nq = seq // t
    nh = width // LANES
    c0 = col0 // LANES
    small = pl.BlockSpec((1, DIFF_HEAD_DIM), lambda b, h, i: (0, 0))
    kernel = functools.partial(_diff_body, t=t, lam_init=lam_init)
    return pl.pallas_call(
        kernel,
        grid=(bsz, nh, nq),
        in_specs=[small, small, small, small,
                  pl.BlockSpec((1, LANES), lambda b, h, i: (0, 0)),
                  pl.BlockSpec((t, LANES), lambda b, h, i: (b * nq + i, c0 + h)),
                  pl.BlockSpec((seq, LANES), lambda b, h, i: (b, c0 + nh + h)),
                  pl.BlockSpec((seq, LANES), lambda b, h, i: (b, c0 + 2 * nh + h))],
        out_specs=pl.BlockSpec((t, LANES), lambda b, h, i: (b * nq + i, h)),
        out_shape=jax.ShapeDtypeStruct((bsz * seq, width), BF16),
        scratch_shapes=[pltpu.VMEM((2, nq, t, t), F32)],
        compiler_params=_params("arbitrary", "arbitrary", "arbitrary"),
        name="diff_attention",
    )(*lam_params, g, proj, proj, proj)


def _s5_body(u_ref, m_ref, ere_ref, eim_ref, fre_ref, fim_ref, are_ref, aim_ref, y_ref,
             xre_s, xim_s, sre_s, sim_s, *, bsz, nchunk, chunk):
    u = jnp.concatenate([u_ref[t] for t in range(chunk)], axis=1)
    _store_wide(xre_s, _dot(u, ere_ref[...]))
    _store_wide(xim_s, _dot(u, eim_ref[...]))
    are = are_ref[...]
    aim = aim_ref[...]
    sre = jnp.zeros((bsz, are.shape[1]), F32)
    sim = jnp.zeros((bsz, are.shape[1]), F32)
    for j in range(nchunk):
        rows = pl.ds(j, bsz, stride=nchunk)
        _store_wide(sre_s, sre, rows)
        _store_wide(sim_s, sim, rows)
        sre, sim = (are * sre - aim * sim + _load_wide(xre_s, rows),
                    are * sim + aim * sre + _load_wide(xim_s, rows))
    y = (_dot(u, m_ref[...])
         + _dot(_load_wide(sre_s).astype(BF16), fre_ref[...])
         + _dot(_load_wide(sim_s).astype(BF16), fim_ref[...]))
    for t in range(chunk):
        y_ref[t] = y[:, t * LANES:(t + 1) * LANES]


def _s5_operators(lam_re, lam_im, b_re, b_im, c_re, c_im, d_skip, log_dt, chunk):
    hp = lax.Precision.HIGHEST
    g, n = lam_re.shape
    ch = b_re.shape[-1]
    dt = jnp.exp(log_dt.astype(F32))[:, None]
    k = jnp.arange(chunk + 1, dtype=F32)[None, :, None]
    mag = jnp.exp(k * (lam_re * dt)[:, None, :])
    ang = k * (lam_im * dt)[:, None, :]
    p_re, p_im = mag * jnp.cos(ang), mag * jnp.sin(ang)
    lb_re, lb_im = p_re[:, 1], p_im[:, 1]
    den = lam_re * lam_re + lam_im * lam_im
    q_re = ((lb_re - 1.0) * lam_re + lb_im * lam_im) / den
    q_im = (lb_im * lam_re - (lb_re - 1.0) * lam_im) / den
    bb_re = q_re[:, :, None] * b_re - q_im[:, :, None] * b_im
    bb_im = q_re[:, :, None] * b_im + q_im[:, :, None] * b_re
    pb_re = p_re[:, :, :, None] * bb_re[:, None] - p_im[:, :, :, None] * bb_im[:, None]
    pb_im = p_re[:, :, :, None] * bb_im[:, None] + p_im[:, :, :, None] * bb_re[:, None]
    w = (jnp.einsum('gon,gkni->gkio', c_re, pb_re[:, :chunk], precision=hp)
         - jnp.einsum('gon,gkni->gkio', c_im, pb_im[:, :chunk], precision=hp))
    w = w.at[:, 0].add(jnp.eye(ch, dtype=F32)[None] * d_skip[:, :, None])
    gpt = LANES // ch
    nt = g // gpt

    def block_diag(x, rdim, cdim):
        rep = jnp.asarray(np.tile(np.eye(cdim, dtype=np.float32), (1, gpt)))
        mask = jnp.asarray((np.arange(gpt * rdim)[:, None] // rdim == np.arange(gpt * cdim)[None, :] // cdim)
                           .astype(np.float32))
        return jnp.einsum('akrc,cl->akrl', x, rep, precision=hp) * mask

    def per_tile(x, perm):
        x = x.reshape(nt, gpt, x.shape[1], x.shape[2], x.shape[3])
        x = x.transpose(0, 2, 1, 4, 3) if perm else x.transpose(0, 2, 1, 3, 4)
        return x.reshape(nt, x.shape[1], gpt * x.shape[3], x.shape[4])

    bd = block_diag(per_tile(w, False), ch, ch).astype(BF16)
    zero = jnp.zeros_like(bd[:, 0])
    m = jnp.concatenate([jnp.concatenate([bd[:, t - s] if t >= s else zero for t in range(chunk)], axis=2)
                         for s in range(chunk)], axis=1)
    to_state = lambda pb: block_diag(per_tile(pb[:, :chunk][:, ::-1], True), ch, n).astype(BF16).reshape(
        nt, chunk * LANES, gpt * n)
    e_re, e_im = to_state(pb_re), to_state(pb_im)
    cp_re = c_re[:, None] * p_re[:, 1:, None, :] - c_im[:, None] * p_im[:, 1:, None, :]
    cp_im = c_re[:, None] * p_im[:, 1:, None, :] + c_im[:, None] * p_re[:, 1:, None, :]

    def from_state(cp):
        blocks = block_diag(per_tile(cp, True), n, ch).astype(BF16)
        return jnp.concatenate([blocks[:, t] for t in range(chunk)], axis=2)

    f_re, f_im = from_state(cp_re), from_state(-cp_im)
    a_re = p_re[:, chunk].reshape(nt, 1, gpt * n)
    a_im = p_im[:, chunk].reshape(nt, 1, gpt * n)
    return m, e_re, e_im, f_re, f_im, a_re, a_im


def _s5_scan(u, ops, bsz, seq):
    chunk, rows, width = u.shape
    nchunk = seq // chunk
    nt = width // LANES
    kdim = chunk * LANES
    sdim = ops[1].shape[-1]
    op = lambda a, b: pl.BlockSpec((None, a, b), lambda i: (i, 0, 0))
    act = pl.BlockSpec((chunk, rows, LANES), lambda i: (0, 0, i))
    return pl.pallas_call(
        functools.partial(_s5_body, bsz=bsz, nchunk=nchunk, chunk=chunk),
        grid=(nt,),
        in_specs=[act, op(kdim, kdim), op(kdim, sdim), op(kdim, sdim), op(sdim, kdim), op(sdim, kdim),
                  op(1, sdim), op(1, sdim)],
        out_specs=act,
        out_shape=jax.ShapeDtypeStruct((chunk, rows, width), F32),
        scratch_shapes=[pltpu.VMEM((sdim // LANES, rows, LANES), F32)] * 4,
        compiler_params=_params("arbitrary"),
        name="s5_scan",
    )(u, *ops)


def _rms(x, g):
    ms = jnp.mean(x * x, axis=-1, keepdims=True)
    return x * lax.rsqrt(ms + EPS) * g


def _outproj_body(ysb_ref, yssm_ref, ydf_ref, x_ref, sbg_ref, wglu_ref, bglu_ref, ssmg_ref, wout_ref,
                  ffng_ref, wrhi_ref, wrlo_ref, rb_ref,
                  h_ref, hn_ref, info_ref, cnt_ref, run_s, yssm_s, *, tm, sbw, ssw):
    step = pl.program_id(0)

    @pl.when(step == 0)
    def _():
        run_s[...] = jnp.zeros_like(run_s)

    ysb = _rms(ysb_ref[...].astype(F32), sbg_ref[...]).astype(BF16)
    chunk = yssm_ref.shape[0]
    for t in range(chunk):
        _store_wide(yssm_s, yssm_ref[t], pl.ds(t, tm // chunk, stride=chunk))
    y = jax.nn.gelu(_load_wide(yssm_s))
    y = y * jax.nn.sigmoid(_dot(y.astype(BF16), wglu_ref[...]) + bglu_ref[...])
    yssm = _rms(y, ssmg_ref[...]).astype(BF16)
    h = (x_ref[...]
         + _dot(ysb, wout_ref[0:sbw, :])
         + _dot(yssm, wout_ref[sbw:sbw + ssw, :])
         + _dot(ydf_ref[...], wout_ref[sbw + ssw:, :]))
    h_ref[...] = h
    hn = _rms(h, ffng_ref[...])
    _store_slabs(hn_ref, hn)

    hi, lo = _split_bf16(hn)
    logits = (_dot(hi, wrhi_ref[...]) + _dot(hi, wrlo_ref[...]) + _dot(lo, wrhi_ref[...])) + rb_ref[...]
    lane = lax.broadcasted_iota(jnp.int32, (tm, LANES), 1).astype(F32)
    ninf = -jnp.inf

    def first_max(v):
        m = jnp.max(v, axis=-1, keepdims=True)
        idx = jnp.min(jnp.where(v == m, lane, float(LANES)), axis=-1, keepdims=True)
        return m, idx

    gl = jnp.where(lane < N_GROUPS, logits, ninf)
    gmax, gidx = first_max(gl)
    g_top = 1.0 / jnp.sum(jnp.exp(gl - gmax), axis=-1, keepdims=True)
    group_lo = N_GROUPS + EXPERTS_PER_GROUP * gidx
    in_group = (lane >= group_lo) & (lane < group_lo + EXPERTS_PER_GROUP)
    el = jnp.where(in_group, logits, ninf)
    m1, i1 = first_max(el)
    m2, i2 = first_max(jnp.where(lane == i1, ninf, el))
    r = jnp.exp(m2 - m1)
    w_a = g_top / (1.0 + r)
    w_b = g_top * r / (1.0 + r)
    e_a = i1 - N_GROUPS
    e_b = i2 - N_GROUPS

    oh_a = (lane == e_a).astype(F32)
    oh_b = (lane == e_b).astype(F32)
    cnt = oh_a + oh_b
    trow = lax.broadcasted_iota(jnp.int32, (tm, tm), 0)
    tcol = lax.broadcasted_iota(jnp.int32, (tm, tm), 1)
    before = (tcol < trow).astype(BF16)
    base = _dot(before, cnt.astype(BF16)) + run_s[...]
    rank_a = jnp.sum(oh_a * base, axis=-1, keepdims=True)
    rank_b = jnp.sum(oh_b * base, axis=-1, keepdims=True)
    run_s[...] = run_s[...] + jnp.sum(cnt, axis=0, keepdims=True)
    cnt_ref[...] = run_s[...]

    info = jnp.zeros((tm, LANES), F32)
    for k, val in ((INFO_EXPERT, e_a), (INFO_EXPERT + 1, e_b), (INFO_GATE, w_a), (INFO_GATE + 1, w_b),
                   (INFO_RANK, rank_a), (INFO_RANK + 1, rank_b)):
        info = jnp.where(lane == k, val, info)
    info_ref[...] = info


def _outproj_router(ysb, yssm, ydf, x, sbg, wglu, bglu, ssmg, wout_all, ffng, wr_hi, wr_lo, rb, layer):
    t, d = x.shape
    tm = min(OUT_TM, t)
    sbw, ssw, dfw = ysb.shape[1], yssm.shape[2], ydf.shape[1]
    chunk = yssm.shape[0]
    rowblk = lambda w: pl.BlockSpec((tm, w), lambda i: (i, 0))
    const = lambda *shape: pl.BlockSpec(shape, lambda i: (0,) * len(shape))
    kernel = functools.partial(_outproj_body, tm=tm, sbw=sbw, ssw=ssw)
    return pl.pallas_call(
        kernel,
        grid=(t // tm,),
        in_specs=[rowblk(sbw), pl.BlockSpec((chunk, tm // chunk, ssw), lambda i: (0, i, 0)), rowblk(dfw), rowblk(d),
                  const(1, sbw), const(ssw, ssw), const(1, ssw), const(1, ssw),
                  pl.BlockSpec((None, d, d), lambda i: (layer, 0, 0)),
                  const(1, d), const(d, LANES), const(d, LANES), const(1, LANES)],
        out_specs=[rowblk(d), pl.BlockSpec((tm * _slab_rows(d), LANES), lambda i: (i, 0)),
                   rowblk(LANES), const(1, LANES)],
        out_shape=[jax.ShapeDtypeStruct((t, d), F32), jax.ShapeDtypeStruct((t * _slab_rows(d), LANES), F32),
                   jax.ShapeDtypeStruct((t, LANES), F32), jax.ShapeDtypeStruct((1, LANES), F32)],
        scratch_shapes=[pltpu.VMEM((1, LANES), F32), pltpu.VMEM((ssw // LANES, tm, LANES), F32)],
        compiler_params=_params("arbitrary"),
        name="outproj_router",
    )(ysb, yssm, ydf, x, sbg, wglu, bglu, ssmg, wout_all, ffng, wr_hi, wr_lo, rb)


def _moe_body(te_ref, nx_ref, nu_ref, src_ref, dst_ref,
              hn_hbm, w1_hbm, w3_hbm, w2_hbm,
              y_hbm,
              xbuf0, xbuf1, ybuf0, ybuf1, w1s, w3s, w2s, w1b, w3b, w2b, gsem, ssem, wsem, *, tm, d, layer):
    i = pl.program_id(0)
    n_used = nu_ref[0]
    xbufs = (xbuf0, xbuf1)
    ybufs = (ybuf0, ybuf1)
    s_rows = _slab_rows(d)

    def weight_copies(e):
        half = w2s.shape[0] // 2
        parts = ((w1_hbm.at[layer, e], w1s, 0), (w3_hbm.at[layer, e], w3s, 1),
                 (w2_hbm.at[layer, e, pl.ds(0, half), :], w2s.at[pl.ds(0, half), :], 0),
                 (w2_hbm.at[layer, e, pl.ds(half, half), :], w2s.at[pl.ds(half, half), :], 1))
        return [(pltpu.make_async_copy(src, dst, wsem.at[k]), queue) for k, (src, dst, queue) in enumerate(parts)]

    def slab(ref, tok):
        return ref.at[pl.ds(pl.multiple_of(tok * s_rows, s_rows), s_rows), :]

    def gather_rows(tile, par, start):
        if not start:
            pltpu.make_async_copy(hn_hbm.at[pl.ds(0, tm * s_rows), :], xbufs[par], gsem.at[par]).wait()
            return
        base = tile * tm
        for r in range(tm):
            pltpu.make_async_copy(slab(hn_hbm, src_ref[base + r]), slab(xbufs[par], r), gsem.at[par]).start()

    def scatter_rows(tile, par, start):
        if not start:
            pltpu.make_async_copy(ybufs[par], y_hbm.at[pl.ds(0, tm * s_rows), :], ssem.at[par]).wait()
            return
        base = (tile + 1) * tm
        for r in range(tm):
            pltpu.make_async_copy(slab(ybufs[par], r), slab(y_hbm, dst_ref[base + r]), ssem.at[par]).start(priority=1)

    @pl.when(i == 0)
    def _():
        ybuf1[...] = jnp.zeros_like(ybuf1)
        for cp, queue in weight_copies(te_ref[0]):
            cp.start(priority=queue)
        gather_rows(0, 0, True)
        gather_rows(0, 0, False)

    new_expert = jnp.logical_or(i == 0, te_ref[i] != te_ref[jnp.maximum(i - 1, 0)])

    @pl.when(jnp.logical_and(i < n_used, new_expert))
    def _():
        for cp, _ in weight_copies(te_ref[i]):
            cp.wait()
        w1b[...] = w1s[...].astype(BF16)
        w3b[...] = w3s[...].astype(BF16)
        w2b[...] = w2s[...].astype(BF16)

        @pl.when(nx_ref[i] >= 0)
        def _():
            for cp, queue in weight_copies(nx_ref[i]):
                cp.start(priority=queue)

    for par in range(2):
        active = jnp.logical_and(i < n_used, i % 2 == par)
        nxt = jnp.minimum(i + 1, n_used - 1)

        @pl.when(active)
        def _(par=par, nxt=nxt):
            x = _load_slabs(xbufs[par], tm, d).astype(BF16)
            gather_rows(nxt, 1 - par, True)
            scatter_rows(i - 1, 1 - par, True)
            h1 = _dot(x, w1b[...])
            h3 = _dot(x, w3b[...])
            a = (h1 * jax.nn.sigmoid(h1) * h3).astype(BF16)
            _store_slabs(ybufs[par], _dot(a, w2b[...]))

        @pl.when(active)
        def _(par=par, nxt=nxt):
            scatter_rows(i - 1, 1 - par, False)
            gather_rows(nxt, 1 - par, False)

        @pl.when(jnp.logical_and(active, i == n_used - 1))
        def _(par=par):
            scatter_rows(i, par, True)
            scatter_rows(i, par, False)


def _moe_experts(hn, tile_expert, next_expert, n_used, src_tok, dst_row, w1_all, w3_all, w2_all, layer):
    d = w1_all.shape[-2]
    s_rows = _slab_rows(d)
    t = hn.shape[0] // s_rows
    tm = MOE_TM
    n_tiles = tile_expert.shape[0]
    f = w1_all.shape[-1]
    hbm = pl.BlockSpec(memory_space=pl.ANY)
    grid_spec = pltpu.PrefetchScalarGridSpec(
        num_scalar_prefetch=5,
        grid=(n_tiles,),
        in_specs=[hbm, hbm, hbm, hbm],
        out_specs=hbm,
        scratch_shapes=[pltpu.VMEM((tm * s_rows, LANES), F32)] * 4
                       + [pltpu.VMEM((d, f), F32), pltpu.VMEM((d, f), F32), pltpu.VMEM((f, d), F32),
                          pltpu.VMEM((d, f), BF16), pltpu.VMEM((d, f), BF16), pltpu.VMEM((f, d), BF16),
                          pltpu.SemaphoreType.DMA((2,)), pltpu.SemaphoreType.DMA((2,)),
                          pltpu.SemaphoreType.DMA((4,))],
    )
    return pl.pallas_call(
        functools.partial(_moe_body, tm=tm, d=d, layer=layer),
        grid_spec=grid_spec,
        out_shape=jax.ShapeDtypeStruct(((2 * t + tm) * s_rows, LANES), F32),
        compiler_params=_params("arbitrary"),
        name="moe_experts",
    )(tile_expert, next_expert, n_used, src_tok, dst_row, hn, w1_all, w3_all, w2_all)


def _moe_schedule(info, counts, t):
    tm = MOE_TM
    n_tiles = (2 * t) // tm + N_EXPERTS
    p = n_tiles * tm
    e_id = info[:, INFO_EXPERT:INFO_EXPERT + 2].astype(jnp.int32)
    rank = info[:, INFO_RANK:INFO_RANK + 2].astype(jnp.int32)
    cnt = counts[0, :N_EXPERTS].astype(jnp.int32)
    tiles_e = (cnt + tm - 1) // tm
    tile_end = jnp.cumsum(tiles_e)
    tile_start = tile_end - tiles_e
    n_used = tile_end[-1]
    pos = (tile_start * tm)[e_id] + rank
    tok = jnp.broadcast_to(jnp.arange(t, dtype=jnp.int32)[:, None], (t, 2))
    dst = tok + jnp.array([0, t], jnp.int32)[None, :]
    trash = 2 * t + jnp.arange(p, dtype=jnp.int32) % tm
    default = jnp.stack([jnp.zeros((p,), jnp.int32), trash], axis=1)
    update = jnp.stack([tok.reshape(-1), dst.reshape(-1)], axis=1)
    table = default.at[pos.reshape(-1)].set(update)
    src_tok = table[:, 0]
    dst_row = jnp.concatenate([trash[:tm], table[:, 1]])
    tile_ids = jnp.minimum(jnp.arange(n_tiles, dtype=jnp.int32), n_used - 1)
    tile_expert = jnp.sum(tile_ids[:, None] >= tile_end[None, :], axis=1).astype(jnp.int32)
    after = tile_end[tile_expert]
    next_expert = jnp.where(after < n_used, tile_expert[jnp.minimum(after, n_tiles - 1)], -1).astype(jnp.int32)
    return tile_expert, next_expert, n_used.reshape(1).astype(jnp.int32), src_tok, dst_row


def _final_body(h_ref, y0_ref, y1_ref, info_ref, g_ref, o_ref):
    o_ref[...] = _rms(_moe_combine(h_ref, y0_ref, y1_ref, info_ref), g_ref[...])


def _final_norm(h, y, info, g):
    t, d = h.shape
    tm = min(FINAL_TM, t)
    nblk = t // tm
    return pl.pallas_call(
        _final_body,
        grid=(nblk,),
        in_specs=[pl.BlockSpec((tm, d), lambda i: (i, 0)),
                  pl.BlockSpec((tm * _slab_rows(d), LANES), lambda i: (i, 0)),
                  pl.BlockSpec((tm * _slab_rows(d), LANES), lambda i: (i + nblk, 0)),
                  pl.BlockSpec((tm, LANES), lambda i: (i, 0)),
                  pl.BlockSpec((1, d), lambda i: (0, 0))],
        out_specs=pl.BlockSpec((tm, d), lambda i: (i, 0)),
        out_shape=jax.ShapeDtypeStruct((t, d), F32),
        compiler_params=_params("arbitrary"),
        name="final_norm",
    )(h, y, y, info, g)


def kernel(x, norm_mix_g, w_in, sb_norm_g, ssm_lam_re, ssm_lam_im, ssm_b_re, ssm_b_im, ssm_c_re, ssm_c_im, ssm_d, ssm_log_dt, ssm_w_glu, ssm_b_glu, ssm_norm_g, diff_lq1, diff_lk1, diff_lq2, diff_lk2, diff_subln_g, w_out, norm_ffn_g, router_group_w, router_group_b, router_expert_w, router_expert_b, expert_w1, expert_w3, expert_w2, final_norm_g):
    bsz, seq, d = x.shape
    depth = w_in.shape[0]
    t = bsz * seq
    sbw = sb_norm_g.shape[-1]
    ssw = ssm_norm_g.shape[-1]
    dfw = d - sbw - ssw
    ssm_col = 3 * sbw
    diff_col = ssm_col + ssw

    w_in_b = w_in.astype(BF16)
    w_out_b = w_out.astype(BF16)
    w_glu_b = ssm_w_glu.astype(BF16)
    row = lambda v: v.reshape(1, -1).astype(F32)

    h = x.reshape(t, d)
    moe_out = None
    for l in range(depth):
        lam_init = 0.8 - 0.6 * math.exp(-0.3 * l)
        proj, ussm, xres = _norm_inproj(h, moe_out, row(norm_mix_g[l]), w_in_b, l, ssm_col, ssw)
        ysb = _sb_attention(proj, bsz, seq, sbw)
        ydf = _diff_attention(proj, [row(p[l]) for p in (diff_lq1, diff_lk1, diff_lq2, diff_lk2)],
                              row(diff_subln_g[l]), bsz, seq, diff_col, dfw, lam_init)
        s5_ops = _s5_operators(ssm_lam_re[l], ssm_lam_im[l], ssm_b_re[l], ssm_b_im[l], ssm_c_re[l], ssm_c_im[l],
                               ssm_d[l], ssm_log_dt[l], S5_CHUNK)
        yssm = _s5_scan(ussm, s5_ops, bsz, seq)

        wr = jnp.zeros((d, LANES), F32)
        wr = wr.at[:, :N_GROUPS].set(router_group_w[l])
        wr = wr.at[:, N_GROUPS:N_GROUPS + N_EXPERTS].set(
            router_expert_w[l].transpose(1, 0, 2).reshape(d, N_EXPERTS))
        wr_hi, wr_lo = _split_bf16(wr)
        rb = jnp.zeros((1, LANES), F32)
        rb = rb.at[0, :N_GROUPS].set(router_group_b[l])
        rb = rb.at[0, N_GROUPS:N_GROUPS + N_EXPERTS].set(router_expert_b[l].reshape(-1))

        h, hn, info, counts = _outproj_router(
            ysb, yssm, ydf, xres, row(sb_norm_g[l]), w_glu_b[l], row(ssm_b_glu[l]), row(ssm_norm_g[l]),
            w_out_b, row(norm_ffn_g[l]), wr_hi, wr_lo, rb, l)
        sched = _moe_schedule(info, counts, t)
        moe_out = (_moe_experts(hn, *sched, expert_w1, expert_w3, expert_w2, l), info)
    out = _final_norm(h, *moe_out, row(final_norm_g))
    return out.reshape(bsz, seq, d)
```

```python
import functools
import math

import numpy as np
import jax
import jax.numpy as jnp
from jax import lax
from jax.experimental import pallas as pl
from jax.experimental.pallas import tpu as pltpu

F32 = jnp.float32
BF16 = jnp.bfloat16

EPS = 1e-6
SB_HEAD_DIM = 64
SSM_GROUP_CH = 16
SSM_STATE = 64
DIFF_HEAD_DIM = 64
N_GROUPS = 4
EXPERTS_PER_GROUP = 8
N_EXPERTS = N_GROUPS * EXPERTS_PER_GROUP

LANES = 128
INFO_EXPERT, INFO_GATE, INFO_RANK = 0, 2, 4
VMEM_LIMIT = 56 * 1024 * 1024

INPROJ_TM = 512
INPROJ_TN = 1024
SB_TQ = 512
SB_TK = 256
DIFF_T = 512
S5_CHUNK = 8
OUT_TM = 256
MOE_TM = 256
FINAL_TM = 256


def _params(*sem):
    return pltpu.CompilerParams(dimension_semantics=sem, vmem_limit_bytes=VMEM_LIMIT)


def _dot(a, b):
    return jnp.dot(a, b, preferred_element_type=F32)


def _dot_nt(a, b):
    return lax.dot_general(a, b, (((1,), (1,)), ((), ())), preferred_element_type=F32)


def _slab_lanes(d):
    return d


def _slab_rows(d):
    return d // _slab_lanes(d)


def _load_slabs(ref, n_tok, d):
    s_rows = _slab_rows(d)
    if s_rows == 1:
        return ref[...]
    return jnp.concatenate([ref[pl.ds(s, n_tok, stride=s_rows), :] for s in range(s_rows)], axis=1)


def _store_slabs(ref, x):
    n_tok, d = x.shape
    s_rows, lanes = _slab_rows(d), _slab_lanes(d)
    if s_rows == 1:
        ref[...] = x
        return
    for s in range(s_rows):
        ref[pl.ds(s, n_tok, stride=s_rows), :] = x[:, s * lanes:(s + 1) * lanes]


def _load_wide(ref, rows=slice(None)):
    return jnp.concatenate([ref[c, rows, :] for c in range(ref.shape[0])], axis=1)


def _store_wide(ref, x, rows=slice(None)):
    for c in range(ref.shape[0]):
        ref[c, rows, :] = x[:, c * LANES:(c + 1) * LANES]


def _split_bf16(x):
    hi = x.astype(BF16)
    lo = (x - hi.astype(F32)).astype(BF16)
    return hi, lo


def _moe_combine(h_ref, y0_ref, y1_ref, info_ref):
    info = info_ref[...]
    n_tok, d = h_ref.shape
    return (h_ref[...] + info[:, INFO_GATE:INFO_GATE + 1] * _load_slabs(y0_ref, n_tok, d)
            + info[:, INFO_GATE + 1:INFO_GATE + 2] * _load_slabs(y1_ref, n_tok, d))


def _inproj_body(*refs, after_moe, ssm_blk, ssm_off, ssm_w, chunk):
    n_in = 4 if after_moe else 1
    g_ref, w_ref = refs[n_in:n_in + 2]
    if after_moe:
        proj_ref, ussm_ref, x_out_ref, xn_ref, u_scr = refs[n_in + 2:]
    else:
        proj_ref, ussm_ref, xn_ref, u_scr = refs[n_in + 2:]

    @pl.when(pl.program_id(1) == 0)
    def _():
        if after_moe:
            x = _moe_combine(*refs[:4])
            x_out_ref[...] = x
        else:
            x = refs[0][...]
        ms = jnp.mean(x * x, axis=-1, keepdims=True)
        xn_ref[...] = (x * lax.rsqrt(ms + EPS) * g_ref[...]).astype(BF16)

    res = _dot(xn_ref[...], w_ref[...])
    proj_ref[...] = res.astype(BF16)

    @pl.when(pl.program_id(1) == ssm_blk)
    def _():
        _store_wide(u_scr, res[:, ssm_off:ssm_off + ssm_w])
        n_chunks = u_scr.shape[1] // chunk
        for t in range(chunk):
            ussm_ref[t] = _load_wide(u_scr, pl.ds(t, n_chunks, stride=chunk)).astype(BF16)


def _norm_inproj(x, moe_out, g, w_all, layer, ssm_col, ssm_w):
    t, d = x.shape
    n = w_all.shape[-1]
    tm, tn = min(INPROJ_TM, t), INPROJ_TN
    chunk = S5_CHUNK
    ssm_blk, ssm_off = divmod(ssm_col, tn)
    assert ssm_off + ssm_w <= tn and tm % chunk == 0
    rows = lambda blk, w: pl.BlockSpec((tm, w), lambda i, j: (i + blk, 0))
    args, in_specs = [x], [rows(0, d)]
    if moe_out is not None:
        y, info = moe_out
        slabs = lambda blk: pl.BlockSpec((tm * _slab_rows(d), _slab_lanes(d)), lambda i, j: (i + blk, 0))
        args += [y, y, info]
        in_specs += [slabs(0), slabs(t // tm), rows(0, LANES)]
    in_specs += [pl.BlockSpec((1, d), lambda i, j: (0, 0)),
                 pl.BlockSpec((None, d, tn), lambda i, j: (layer, 0, j))]
    out_shape = [jax.ShapeDtypeStruct((t, n), BF16), jax.ShapeDtypeStruct((chunk, t // chunk, ssm_w), BF16)]
    out_specs = [pl.BlockSpec((tm, tn), lambda i, j: (i, j)),
                 pl.BlockSpec((chunk, tm // chunk, ssm_w), lambda i, j: (0, i, 0))]
    if moe_out is not None:
        out_shape.append(jax.ShapeDtypeStruct((t, d), F32))
        out_specs.append(pl.BlockSpec((tm, d), lambda i, j: (i, 0)))
    outs = pl.pallas_call(
        functools.partial(_inproj_body, after_moe=moe_out is not None, ssm_blk=ssm_blk, ssm_off=ssm_off,
                          ssm_w=ssm_w, chunk=chunk),
        grid=(t // tm, n // tn),
        in_specs=in_specs, out_specs=out_specs, out_shape=out_shape,
        scratch_shapes=[pltpu.VMEM((tm, d), BF16), pltpu.VMEM((ssm_w // LANES, tm, LANES), F32)],
        compiler_params=_params("arbitrary", "arbitrary"),
        name="norm_inproj",
    )(*args, g, w_all)
    return outs if moe_out is not None else (outs[0], outs[1], x)


def _sb_body(q_ref, k_ref, v_ref, o_ref, *, tq, tk):
    qi = pl.program_id(2)
    hd = SB_HEAD_DIM
    nsub = tq // tk
    lane = lax.broadcasted_iota(jnp.int32, (1, 2 * hd), 1)
    head_lanes = (lane < hd, lane >= hd)
    qs = q_ref[...] * (hd ** -0.5)
    zero = jnp.zeros((), BF16)
    qm = [jnp.where(m, qs, zero) for m in head_lanes]
    later = (lax.broadcasted_iota(jnp.int32, (tk, tk), 0)
             > lax.broadcasted_iota(jnp.int32, (tk, tk), 1)).astype(BF16)
    q_pos = qi * tq + lax.broadcasted_iota(jnp.int32, (tq, tk), 0)
    k_off = lax.broadcasted_iota(jnp.int32, (tq, tk), 1)

    def block(kb, carry, diag, r0=0):
        acc, runs = carry
        start = pl.multiple_of(kb * tk, tk)
        kblk = k_ref[pl.ds(start, tk), :]
        vblk = v_ref[pl.ds(start, tk), :]
        if diag:
            strict = (kb * tk + k_off < q_pos)[r0:]
        new_runs = []
        for h in range(2):
            z = _dot_nt(qm[h][r0:], kblk)
            log_fail = -jnp.maximum(z, 0.0) - jnp.log(1.0 + jnp.exp(-jnp.abs(z)))
            log_hit = log_fail + z
            if diag:
                log_fail = jnp.where(strict, log_fail, 0.0)
            log_after = _dot(log_fail.astype(BF16), later)
            w = jnp.exp(log_hit + log_after + runs[h][r0:])
            if diag:
                w = jnp.where(strict, w, 0.0)
            vm = jnp.where(head_lanes[h], vblk, zero)
            upd = _dot(w.astype(BF16), vm)
            run = runs[h][r0:] + jnp.sum(log_fail, axis=-1, keepdims=True)
            if r0:
                acc = jnp.concatenate([acc[:r0], acc[r0:] + upd], axis=0)
                run = jnp.concatenate([runs[h][:r0], run], axis=0)
            else:
                acc = acc + upd
            new_runs.append(run)
        return acc, tuple(new_runs)

    zrun = jnp.zeros((tq, 1), F32)
    carry = (jnp.zeros((tq, 2 * hd), F32), (zrun, zrun))
    for sub in reversed(range(nsub)):
        carry = block(nsub * qi + sub, carry, True, r0=sub * tk)
    def earlier_tile(it, c):
        for u in range(nsub):
            c = block(nsub * (qi - it) - 1 - u, c, False)
        return c

    carry = lax.fori_loop(0, qi, earlier_tile, carry)
    o_ref[...] = carry[0].astype(BF16)


def _sb_attention(proj, bsz, seq, width):
    tq, tk = min(SB_TQ, seq), min(SB_TK, seq)
    nq = seq // tq
    npair = width // LANES
    kernel = functools.partial(_sb_body, tq=tq, tk=tk)
    return pl.pallas_call(
        kernel,
        grid=(bsz, npair, nq),
        in_specs=[pl.BlockSpec((tq, LANES), lambda b, p, i: (b * nq + i, p)),
                  pl.BlockSpec((seq, LANES), lambda b, p, i: (b, npair + p)),
                  pl.BlockSpec((seq, LANES), lambda b, p, i: (b, 2 * npair + p))],
        out_specs=pl.BlockSpec((tq, LANES), lambda b, p, i: (b * nq + i, p)),
        out_shape=jax.ShapeDtypeStruct((bsz * seq, width), BF16),
        compiler_params=_params("arbitrary", "arbitrary", "arbitrary"),
        name="sb_attention",
    )(proj, proj, proj)


def _diff_body(lq1_ref, lk1_ref, lq2_ref, lk2_ref, g_ref, q_ref, k_ref, v_ref, o_ref, s_scr, *, t, lam_init):
    qi = pl.program_id(2)
    hd = DIFF_HEAD_DIM
    lam = (jnp.exp(jnp.sum(lq1_ref[...] * lk1_ref[...], axis=-1, keepdims=True))
           - jnp.exp(jnp.sum(lq2_ref[...] * lk2_ref[...], axis=-1, keepdims=True)) + lam_init)
    lane = lax.broadcasted_iota(jnp.int32, (1, 2 * hd), 1)
    qs = q_ref[...] * (hd ** -0.5)
    zero = jnp.zeros((), BF16)
    qm = [jnp.where(lane < hd, qs, zero), jnp.where(lane >= hd, qs, zero)]
    row = lax.broadcasted_iota(jnp.int32, (t, t), 0)
    col = lax.broadcasted_iota(jnp.int32, (t, t), 1)
    causal = col <= row

    hs = t // 2

    def lane_max(acc, s):
        for c in range(s.shape[1] // LANES):
            acc = jnp.maximum(acc, s[:, c * LANES:(c + 1) * LANES])
        return acc

    def score_block(kb, mx):
        kblk = k_ref[pl.ds(pl.multiple_of(kb * t, t), t), :]
        out = []
        for h in range(2):
            s = _dot_nt(qm[h], kblk)
            s_scr[h, kb] = s
            out.append(lane_max(mx[h], s))
        return tuple(out)

    def diag_scores():
        kblk = k_ref[pl.ds(pl.multiple_of(qi * t, t), t), :]
        out = []
        for h in range(2):
            s_left = jnp.where(causal[:, :hs], _dot_nt(qm[h], kblk[:hs]), -jnp.inf)
            s_right = jnp.where(causal[hs:, hs:], _dot_nt(qm[h][hs:], kblk[hs:]), -jnp.inf)
            s_scr[h, qi, :, pl.ds(0, hs)] = s_left
            s_scr[h, qi, pl.ds(hs, hs), pl.ds(hs, hs)] = s_right
            smax = lane_max(jnp.full((t, LANES), -jnp.inf, F32), s_left)
            out.append(jnp.concatenate([smax[:hs], lane_max(smax[hs:], s_right)], axis=0))
        return tuple(out)

    mx = lax.fori_loop(0, qi, score_block, diag_scores())
    m = [jnp.broadcast_to(jnp.max(mx[h], axis=-1, keepdims=True), (t, t)) for h in range(2)]
    ones = jnp.ones((t, LANES), BF16)

    def pv_block(kb, acc):
        vaug = jnp.concatenate([v_ref[pl.ds(pl.multiple_of(kb * t, t), t), :], ones], axis=1)
        return tuple(acc[h] + _dot(jnp.exp(s_scr[h, kb] - m[h]).astype(BF16), vaug) for h in range(2))

    def pv_diag(acc):
        vaug = jnp.concatenate([v_ref[pl.ds(pl.multiple_of(qi * t, t), t), :], ones], axis=1)
        out = []
        for h in range(2):
            p_left = jnp.exp(s_scr[h, qi, :, pl.ds(0, hs)] - m[h][:, :hs]).astype(BF16)
            p_right = jnp.exp(s_scr[h, qi, pl.ds(hs, hs), pl.ds(hs, hs)] - m[h][hs:, :hs]).astype(BF16)
            a = acc[h] + _dot(p_left, vaug[:hs])
            out.append(jnp.concatenate([a[:hs], a[hs:] + _dot(p_right, vaug[hs:])], axis=0))
        return tuple(out)

    acc0 = jnp.zeros((t, 2 * LANES), F32)
    a0, a1 = pv_diag(lax.fori_loop(0, qi, pv_block, (acc0, acc0)))
    o = a0[:, :LANES] / a0[:, LANES:LANES + 1] - lam * (a1[:, :LANES] / a1[:, LANES:LANES + 1])
    ms = jnp.mean(o * o, axis=-1, keepdims=True)
    o_ref[...] = (o * lax.rsqrt(ms + EPS) * g_ref[...] * (1.0 - lam_init)).astype(BF16)


def _diff_attention(proj, lam_params, g, bsz, seq, col0, width, lam_init):
    t = min(DIFF_T, seq)
    nq = seq // t
    nh = width // LANES
    c0 = col0 // LANES
    small = pl.BlockSpec((1, DIFF_HEAD_DIM), lambda b, h, i: (0, 0))
    kernel = functools.partial(_diff_body, t=t, lam_init=lam_init)
    return pl.pallas_call(
        kernel,
        grid=(bsz, nh, nq),
        in_specs=[small, small, small, small,
                  pl.BlockSpec((1, LANES), lambda b, h, i: (0, 0)),
                  pl.BlockSpec((t, LANES), lambda b, h, i: (b * nq + i, c0 + h)),
                  pl.BlockSpec((seq, LANES), lambda b, h, i: (b, c0 + nh + h)),
                  pl.BlockSpec((seq, LANES), lambda b, h, i: (b, c0 + 2 * nh + h))],
        out_specs=pl.BlockSpec((t, LANES), lambda b, h, i: (b * nq + i, h)),
        out_shape=jax.ShapeDtypeStruct((bsz * seq, width), BF16),
        scratch_shapes=[pltpu.VMEM((2, nq, t, t), F32)],
        compiler_params=_params("arbitrary", "arbitrary", "arbitrary"),
        name="diff_attention",
    )(*lam_params, g, proj, proj, proj)


def _s5_body(u_ref, m_ref, ere_ref, eim_ref, fre_ref, fim_ref, are_ref, aim_ref, y_ref,
             xre_s, xim_s, sre_s, sim_s, *, bsz, nchunk, chunk):
    u = jnp.concatenate([u_ref[t] for t in range(chunk)], axis=1)
    _store_wide(xre_s, _dot(u, ere_ref[...]))
    _store_wide(xim_s, _dot(u, eim_ref[...]))
    are = are_ref[...]
    aim = aim_ref[...]
    sre = jnp.zeros((bsz, are.shape[1]), F32)
    sim = jnp.zeros((bsz, are.shape[1]), F32)
    for j in range(nchunk):
        rows = pl.ds(j, bsz, stride=nchunk)
        _store_wide(sre_s, sre, rows)
        _store_wide(sim_s, sim, rows)
        sre, sim = (are * sre - aim * sim + _load_wide(xre_s, rows),
                    are * sim + aim * sre + _load_wide(xim_s, rows))
    y = (_dot(u, m_ref[...])
         + _dot(_load_wide(sre_s).astype(BF16), fre_ref[...])
         + _dot(_load_wide(sim_s).astype(BF16), fim_ref[...]))
    for t in range(chunk):
        y_ref[t] = y[:, t * LANES:(t + 1) * LANES]


def _s5_operators(lam_re, lam_im, b_re, b_im, c_re, c_im, d_skip, log_dt, chunk):
    hp = lax.Precision.HIGHEST
    g, n = lam_re.shape
    ch = b_re.shape[-1]
    dt = jnp.exp(log_dt.astype(F32))[:, None]
    k = jnp.arange(chunk + 1, dtype=F32)[None, :, None]
    mag = jnp.exp(k * (lam_re * dt)[:, None, :])
    ang = k * (lam_im * dt)[:, None, :]
    p_re, p_im = mag * jnp.cos(ang), mag * jnp.sin(ang)
    lb_re, lb_im = p_re[:, 1], p_im[:, 1]
    den = lam_re * lam_re + lam_im * lam_im
    q_re = ((lb_re - 1.0) * lam_re + lb_im * lam_im) / den
    q_im = (lb_im * lam_re - (lb_re - 1.0) * lam_im) / den
    bb_re = q_re[:, :, None] * b_re - q_im[:, :, None] * b_im
    bb_im = q_re[:, :, None] * b_im + q_im[:, :, None] * b_re
    pb_re = p_re[:, :, :, None] * bb_re[:, None] - p_im[:, :, :, None] * bb_im[:, None]
    pb_im = p_re[:, :, :, None] * bb_im[:, None] + p_im[:, :, :, None] * bb_re[:, None]
    w = (jnp.einsum('gon,gkni->gkio', c_re, pb_re[:, :chunk], precision=hp)
         - jnp.einsum('gon,gkni->gkio', c_im, pb_im[:, :chunk], precision=hp))
    w = w.at[:, 0].add(jnp.eye(ch, dtype=F32)[None] * d_skip[:, :, None])
    gpt = LANES // ch
    nt = g // gpt

    def block_diag(x, rdim, cdim):
        rep = jnp.asarray(np.tile(np.eye(cdim, dtype=np.float32), (1, gpt)))
        mask = jnp.asarray((np.arange(gpt * rdim)[:, None] // rdim == np.arange(gpt * cdim)[None, :] // cdim)
                           .astype(np.float32))
        return jnp.einsum('akrc,cl->akrl', x, rep, precision=hp) * mask

    def per_tile(x, perm):
        x = x.reshape(nt, gpt, x.shape[1], x.shape[2], x.shape[3])
        x = x.transpose(0, 2, 1, 4, 3) if perm else x.transpose(0, 2, 1, 3, 4)
        return x.reshape(nt, x.shape[1], gpt * x.shape[3], x.shape[4])

    bd = block_diag(per_tile(w, False), ch, ch).astype(BF16)
    zero = jnp.zeros_like(bd[:, 0])
    m = jnp.concatenate([jnp.concatenate([bd[:, t - s] if t >= s else zero for t in range(chunk)], axis=2)
                         for s in range(chunk)], axis=1)
    to_state = lambda pb: block_diag(per_tile(pb[:, :chunk][:, ::-1], True), ch, n).astype(BF16).reshape(
        nt, chunk * LANES, gpt * n)
    e_re, e_im = to_state(pb_re), to_state(pb_im)
    cp_re = c_re[:, None] * p_re[:, 1:, None, :] - c_im[:, None] * p_im[:, 1:, None, :]
    cp_im = c_re[:, None] * p_im[:, 1:, None, :] + c_im[:, None] * p_re[:, 1:, None, :]

    def from_state(cp):
        blocks = block_diag(per_tile(cp, True), n, ch).astype(BF16)
        return jnp.concatenate([blocks[:, t] for t in range(chunk)], axis=2)

    f_re, f_im = from_state(cp_re), from_state(-cp_im)
    a_re = p_re[:, chunk].reshape(nt, 1, gpt * n)
    a_im = p_im[:, chunk].reshape(nt, 1, gpt * n)
    return m, e_re, e_im, f_re, f_im, a_re, a_im


def _s5_scan(u, ops, bsz, seq):
    chunk, rows, width = u.shape
    nchunk = seq // chunk
    nt = width // LANES
    kdim = chunk * LANES
    sdim = ops[1].shape[-1]
    op = lambda a, b: pl.BlockSpec((None, a, b), lambda i: (i, 0, 0))
    act = pl.BlockSpec((chunk, rows, LANES), lambda i: (0, 0, i))
    return pl.pallas_call(
        functools.partial(_s5_body, bsz=bsz, nchunk=nchunk, chunk=chunk),
        grid=(nt,),
        in_specs=[act, op(kdim, kdim), op(kdim, sdim), op(kdim, sdim), op(sdim, kdim), op(sdim, kdim),
                  op(1, sdim), op(1, sdim)],
        out_specs=act,
        out_shape=jax.ShapeDtypeStruct((chunk, rows, width), F32),
        scratch_shapes=[pltpu.VMEM((sdim // LANES, rows, LANES), F32)] * 4,
        compiler_params=_params("arbitrary"),
        name="s5_scan",
    )(u, *ops)


def _rms(x, g):
    ms = jnp.mean(x * x, axis=-1, keepdims=True)
    return x * lax.rsqrt(ms + EPS) * g


def _outproj_body(ysb_ref, yssm_ref, ydf_ref, x_ref, sbg_ref, wglu_ref, bglu_ref, ssmg_ref, wout_ref,
                  ffng_ref, wrhi_ref, wrlo_ref, rb_ref,
                  h_ref, hn_ref, info_ref, cnt_ref, run_s, yssm_s, *, tm, sbw, ssw):
    step = pl.program_id(0)

    @pl.when(step == 0)
    def _():
        run_s[...] = jnp.zeros_like(run_s)

    ysb = _rms(ysb_ref[...].astype(F32), sbg_ref[...]).astype(BF16)
    chunk = yssm_ref.shape[0]
    for t in range(chunk):
        _store_wide(yssm_s, yssm_ref[t], pl.ds(t, tm // chunk, stride=chunk))
    y = jax.nn.gelu(_load_wide(yssm_s))
    y = y * jax.nn.sigmoid(_dot(y.astype(BF16), wglu_ref[...]) + bglu_ref[...])
    yssm = _rms(y, ssmg_ref[...]).astype(BF16)
    h = (x_ref[...]
         + _dot(ysb, wout_ref[0:sbw, :])
         + _dot(yssm, wout_ref[sbw:sbw + ssw, :])
         + _dot(ydf_ref[...], wout_ref[sbw + ssw:, :]))
    h_ref[...] = h
    hn = _rms(h, ffng_ref[...])
    _store_slabs(hn_ref, hn)

    hi, lo = _split_bf16(hn)
    logits = (_dot(hi, wrhi_ref[...]) + _dot(hi, wrlo_ref[...]) + _dot(lo, wrhi_ref[...])) + rb_ref[...]
    lane = lax.broadcasted_iota(jnp.int32, (tm, LANES), 1).astype(F32)
    ninf = -jnp.inf

    def first_max(v):
        m = jnp.max(v, axis=-1, keepdims=True)
        idx = jnp.min(jnp.where(v == m, lane, float(LANES)), axis=-1, keepdims=True)
        return m, idx

    gl = jnp.where(lane < N_GROUPS, logits, ninf)
    gmax, gidx = first_max(gl)
    g_top = 1.0 / jnp.sum(jnp.exp(gl - gmax), axis=-1, keepdims=True)
    group_lo = N_GROUPS + EXPERTS_PER_GROUP * gidx
    in_group = (lane >= group_lo) & (lane < group_lo + EXPERTS_PER_GROUP)
    el = jnp.where(in_group, logits, ninf)
    m1, i1 = first_max(el)
    m2, i2 = first_max(jnp.where(lane == i1, ninf, el))
    r = jnp.exp(m2 - m1)
    w_a = g_top / (1.0 + r)
    w_b = g_top * r / (1.0 + r)
    e_a = i1 - N_GROUPS
    e_b = i2 - N_GROUPS

    oh_a = (lane == e_a).astype(F32)
    oh_b = (lane == e_b).astype(F32)
    cnt = oh_a + oh_b
    trow = lax.broadcasted_iota(jnp.int32, (tm, tm), 0)
    tcol = lax.broadcasted_iota(jnp.int32, (tm, tm), 1)
    before = (tcol < trow).astype(BF16)
    base = _dot(before, cnt.astype(BF16)) + run_s[...]
    rank_a = jnp.sum(oh_a * base, axis=-1, keepdims=True)
    rank_b = jnp.sum(oh_b * base, axis=-1, keepdims=True)
    run_s[...] = run_s[...] + jnp.sum(cnt, axis=0, keepdims=True)
    cnt_ref[...] = run_s[...]

    info = jnp.zeros((tm, LANES), F32)
    for k, val in ((INFO_EXPERT, e_a), (INFO_EXPERT + 1, e_b), (INFO_GATE, w_a), (INFO_GATE + 1, w_b),
                   (INFO_RANK, rank_a), (INFO_RANK + 1, rank_b)):
        info = jnp.where(lane == k, val, info)
    info_ref[...] = info


def _outproj_router(ysb, yssm, ydf, x, sbg, wglu, bglu, ssmg, wout_all, ffng, wr_hi, wr_lo, rb, layer):
    t, d = x.shape
    tm = min(OUT_TM, t)
    sbw, ssw, dfw = ysb.shape[1], yssm.shape[2], ydf.shape[1]
    chunk = yssm.shape[0]
    rowblk = lambda w: pl.BlockSpec((tm, w), lambda i: (i, 0))
    const = lambda *shape: pl.BlockSpec(shape, lambda i: (0,) * len(shape))
    kernel = functools.partial(_outproj_body, tm=tm, sbw=sbw, ssw=ssw)
    return pl.pallas_call(
        kernel,
        grid=(t // tm,),
        in_specs=[rowblk(sbw), pl.BlockSpec((chunk, tm // chunk, ssw), lambda i: (0, i, 0)), rowblk(dfw), rowblk(d),
                  const(1, sbw), const(ssw, ssw), const(1, ssw), const(1, ssw),
                  pl.BlockSpec((None, d, d), lambda i: (layer, 0, 0)),
                  const(1, d), const(d, LANES), const(d, LANES), const(1, LANES)],
        out_specs=[rowblk(d), pl.BlockSpec((tm * _slab_rows(d), _slab_lanes(d)), lambda i: (i, 0)),
                   rowblk(LANES), const(1, LANES)],
        out_shape=[jax.ShapeDtypeStruct((t, d), F32), jax.ShapeDtypeStruct((t * _slab_rows(d), _slab_lanes(d)), F32),
                   jax.ShapeDtypeStruct((t, LANES), F32), jax.ShapeDtypeStruct((1, LANES), F32)],
        scratch_shapes=[pltpu.VMEM((1, LANES), F32), pltpu.VMEM((ssw // LANES, tm, LANES), F32)],
        compiler_params=_params("arbitrary"),
        name="outproj_router",
    )(ysb, yssm, ydf, x, sbg, wglu, bglu, ssmg, wout_all, ffng, wr_hi, wr_lo, rb)


def _moe_body(te_ref, nx_ref, nu_ref, src_ref, dst_ref,
              hn_hbm, w1_hbm, w3_hbm, w2_hbm,
              y_hbm,
              xbuf0, xbuf1, ybuf0, ybuf1, w1s, w3s, w2s, w1b, w3b, w2b, gsem, ssem, wsem, *, tm, d, layer):
    i = pl.program_id(0)
    n_used = nu_ref[0]
    xbufs = (xbuf0, xbuf1)
    ybufs = (ybuf0, ybuf1)
    s_rows = _slab_rows(d)

    def weight_copies(e):
        half = w2s.shape[0] // 2
        parts = ((w1_hbm.at[layer, e], w1s, 0), (w3_hbm.at[layer, e], w3s, 1),
                 (w2_hbm.at[layer, e, pl.ds(0, half), :], w2s.at[pl.ds(0, half), :], 0),
                 (w2_hbm.at[layer, e, pl.ds(half, half), :], w2s.at[pl.ds(half, half), :], 1))
        return [(pltpu.make_async_copy(src, dst, wsem.at[k]), queue) for k, (src, dst, queue) in enumerate(parts)]

    def slab(ref, tok):
        return ref.at[pl.ds(pl.multiple_of(tok * s_rows, s_rows), s_rows), :]

    def gather_rows(tile, par, start):
        if not start:
            pltpu.make_async_copy(hn_hbm.at[pl.ds(0, tm * s_rows), :], xbufs[par], gsem.at[par]).wait()
            return
        base = tile * tm
        for r in range(tm):
            pltpu.make_async_copy(slab(hn_hbm, src_ref[base + r]), slab(xbufs[par], r), gsem.at[par]).start()

    def scatter_rows(tile, par, start):
        if not start:
            pltpu.make_async_copy(ybufs[par], y_hbm.at[pl.ds(0, tm * s_rows), :], ssem.at[par]).wait()
            return
        base = (tile + 1) * tm
        for r in range(tm):
            pltpu.make_async_copy(slab(ybufs[par], r), slab(y_hbm, dst_ref[base + r]), ssem.at[par]).start(priority=1)

    @pl.when(i == 0)
    def _():
        ybuf1[...] = jnp.zeros_like(ybuf1)
        for cp, queue in weight_copies(te_ref[0]):
            cp.start(priority=queue)
        gather_rows(0, 0, True)
        gather_rows(0, 0, False)

    new_expert = jnp.logical_or(i == 0, te_ref[i] != te_ref[jnp.maximum(i - 1, 0)])

    @pl.when(jnp.logical_and(i < n_used, new_expert))
    def _():
        for cp, _ in weight_copies(te_ref[i]):
            cp.wait()
        w1b[...] = w1s[...].astype(BF16)
        w3b[...] = w3s[...].astype(BF16)
        w2b[...] = w2s[...].astype(BF16)

        @pl.when(nx_ref[i] >= 0)
        def _():
            for cp, queue in weight_copies(nx_ref[i]):
                cp.start(priority=queue)

    for par in range(2):
        active = jnp.logical_and(i < n_used, i % 2 == par)
        nxt = jnp.minimum(i + 1, n_used - 1)

        @pl.when(active)
        def _(par=par, nxt=nxt):
            x = _load_slabs(xbufs[par], tm, d).astype(BF16)
            gather_rows(nxt, 1 - par, True)
            scatter_rows(i - 1, 1 - par, True)
            h1 = _dot(x, w1b[...])
            h3 = _dot(x, w3b[...])
            a = (h1 * jax.nn.sigmoid(h1) * h3).astype(BF16)
            _store_slabs(ybufs[par], _dot(a, w2b[...]))

        @pl.when(active)
        def _(par=par, nxt=nxt):
            scatter_rows(i - 1, 1 - par, False)
            gather_rows(nxt, 1 - par, False)

        @pl.when(jnp.logical_and(active, i == n_used - 1))
        def _(par=par):
            scatter_rows(i, par, True)
            scatter_rows(i, par, False)


def _moe_experts(hn, tile_expert, next_expert, n_used, src_tok, dst_row, w1_all, w3_all, w2_all, layer):
    d = w1_all.shape[-2]
    s_rows = _slab_rows(d)
    t = hn.shape[0] // s_rows
    tm = MOE_TM
    n_tiles = tile_expert.shape[0]
    f = w1_all.shape[-1]
    hbm = pl.BlockSpec(memory_space=pl.ANY)
    grid_spec = pltpu.PrefetchScalarGridSpec(
        num_scalar_prefetch=5,
        grid=(n_tiles,),
        in_specs=[hbm, hbm, hbm, hbm],
        out_specs=hbm,
        scratch_shapes=[pltpu.VMEM((tm * s_rows, _slab_lanes(d)), F32)] * 4
                       + [pltpu.VMEM((d, f), F32), pltpu.VMEM((d, f), F32), pltpu.VMEM((f, d), F32),
                          pltpu.VMEM((d, f), BF16), pltpu.VMEM((d, f), BF16), pltpu.VMEM((f, d), BF16),
                          pltpu.SemaphoreType.DMA((2,)), pltpu.SemaphoreType.DMA((2,)),
                          pltpu.SemaphoreType.DMA((4,))],
    )
    return pl.pallas_call(
        functools.partial(_moe_body, tm=tm, d=d, layer=layer),
        grid_spec=grid_spec,
        out_shape=jax.ShapeDtypeStruct(((2 * t + tm) * s_rows, _slab_lanes(d)), F32),
        compiler_params=_params("arbitrary"),
        name="moe_experts",
    )(tile_expert, next_expert, n_used, src_tok, dst_row, hn, w1_all, w3_all, w2_all)


def _moe_schedule(info, counts, t):
    tm = MOE_TM
    n_tiles = (2 * t) // tm + N_EXPERTS
    p = n_tiles * tm
    e_id = info[:, INFO_EXPERT:INFO_EXPERT + 2].astype(jnp.int32)
    rank = info[:, INFO_RANK:INFO_RANK + 2].astype(jnp.int32)
    cnt = counts[0, :N_EXPERTS].astype(jnp.int32)
    tiles_e = (cnt + tm - 1) // tm
    tile_end = jnp.cumsum(tiles_e)
    tile_start = tile_end - tiles_e
    n_used = tile_end[-1]
    pos = (tile_start * tm)[e_id] + rank
    tok = jnp.broadcast_to(jnp.arange(t, dtype=jnp.int32)[:, None], (t, 2))
    dst = tok + jnp.array([0, t], jnp.int32)[None, :]
    trash = 2 * t + jnp.arange(p, dtype=jnp.int32) % tm
    default = jnp.stack([jnp.zeros((p,), jnp.int32), trash], axis=1)
    update = jnp.stack([tok.reshape(-1), dst.reshape(-1)], axis=1)
    table = default.at[pos.reshape(-1)].set(update)
    src_tok = table[:, 0]
    dst_row = jnp.concatenate([trash[:tm], table[:, 1]])
    tile_ids = jnp.minimum(jnp.arange(n_tiles, dtype=jnp.int32), n_used - 1)
    tile_expert = jnp.sum(tile_ids[:, None] >= tile_end[None, :], axis=1).astype(jnp.int32)
    after = tile_end[tile_expert]
    next_expert = jnp.where(after < n_used, tile_expert[jnp.minimum(after, n_tiles - 1)], -1).astype(jnp.int32)
    return tile_expert, next_expert, n_used.reshape(1).astype(jnp.int32), src_tok, dst_row


def _final_body(h_ref, y0_ref, y1_ref, info_ref, g_ref, o_ref):
    o_ref[...] = _rms(_moe_combine(h_ref, y0_ref, y1_ref, info_ref), g_ref[...])


def _final_norm(h, y, info, g):
    t, d = h.shape
    tm = min(FINAL_TM, t)
    nblk = t // tm
    return pl.pallas_call(
        _final_body,
        grid=(nblk,),
        in_specs=[pl.BlockSpec((tm, d), lambda i: (i, 0)),
                  pl.BlockSpec((tm * _slab_rows(d), _slab_lanes(d)), lambda i: (i, 0)),
                  pl.BlockSpec((tm * _slab_rows(d), _slab_lanes(d)), lambda i: (i + nblk, 0)),
                  pl.BlockSpec((tm, LANES), lambda i: (i, 0)),
                  pl.BlockSpec((1, d), lambda i: (0, 0))],
        out_specs=pl.BlockSpec((tm, d), lambda i: (i, 0)),
        out_shape=jax.ShapeDtypeStruct((t, d), F32),
        compiler_params=_params("arbitrary"),
        name="final_norm",
    )(h, y, y, info, g)


def kernel(x, norm_mix_g, w_in, sb_norm_g, ssm_lam_re, ssm_lam_im, ssm_b_re, ssm_b_im, ssm_c_re, ssm_c_im, ssm_d, ssm_log_dt, ssm_w_glu, ssm_b_glu, ssm_norm_g, diff_lq1, diff_lk1, diff_lq2, diff_lk2, diff_subln_g, w_out, norm_ffn_g, router_group_w, router_group_b, router_expert_w, router_expert_b, expert_w1, expert_w3, expert_w2, final_norm_g):
    bsz, seq, d = x.shape
    depth = w_in.shape[0]
    t = bsz * seq
    sbw = sb_norm_g.shape[-1]
    ssw = ssm_norm_g.shape[-1]
    dfw = d - sbw - ssw
    ssm_col = 3 * sbw
    diff_col = ssm_col + ssw

    w_in_b = w_in.astype(BF16)
    w_out_b = w_out.astype(BF16)
    w_glu_b = ssm_w_glu.astype(BF16)
    row = lambda v: v.reshape(1, -1).astype(F32)

    h = x.reshape(t, d)
    moe_out = None
    for l in range(depth):
        lam_init = 0.8 - 0.6 * math.exp(-0.3 * l)
        proj, ussm, xres = _norm_inproj(h, moe_out, row(norm_mix_g[l]), w_in_b, l, ssm_col, ssw)
        ysb = _sb_attention(proj, bsz, seq, sbw)
        ydf = _diff_attention(proj, [row(p[l]) for p in (diff_lq1, diff_lk1, diff_lq2, diff_lk2)],
                              row(diff_subln_g[l]), bsz, seq, diff_col, dfw, lam_init)
        s5_ops = _s5_operators(ssm_lam_re[l], ssm_lam_im[l], ssm_b_re[l], ssm_b_im[l], ssm_c_re[l], ssm_c_im[l],
                               ssm_d[l], ssm_log_dt[l], S5_CHUNK)
        yssm = _s5_scan(ussm, s5_ops, bsz, seq)

        wr = jnp.zeros((d, LANES), F32)
        wr = wr.at[:, :N_GROUPS].set(router_group_w[l])
        wr = wr.at[:, N_GROUPS:N_GROUPS + N_EXPERTS].set(
            router_expert_w[l].transpose(1, 0, 2).reshape(d, N_EXPERTS))
        wr_hi, wr_lo = _split_bf16(wr)
        rb = jnp.zeros((1, LANES), F32)
        rb = rb.at[0, :N_GROUPS].set(router_group_b[l])
        rb = rb.at[0, N_GROUPS:N_GROUPS + N_EXPERTS].set(router_expert_b[l].reshape(-1))

        h, hn, info, counts = _outproj_router(
            ysb, yssm, ydf, xres, row(sb_norm_g[l]), w_glu_b[l], row(ssm_b_glu[l]), row(ssm_norm_g[l]),
            w_out_b, row(norm_ffn_g[l]), wr_hi, wr_lo, rb, l)
        sched = _moe_schedule(info, counts, t)
        moe_out = (_moe_experts(hn, *sched, expert_w1, expert_w3, expert_w2, l), info)
    out = _final_norm(h, *moe_out, row(final_norm_g))
    return out.reshape(bsz, seq, d)
```
